```python
import math
import jax, jax.numpy as jnp
from jax import lax
import numpy as np

D_MODEL = 1024
BATCH = 8
SEQ = 4096
DEPTH = 1

DN_HEADS = 4
DN_DK = 128
DN_DV = 128
CONV_K = 4
CHUNK = 64
MLA_HEADS = 4
QK_NOPE = 128
QK_ROPE = 64
V_HEAD = 128
Q_LORA = 512
KV_LORA = 256
ROPE_THETA = 10000.0
Q_BLOCK = 128
D_FF = -(-8 * D_MODEL // (3 * 256)) * 256

DN_QK = DN_HEADS * DN_DK
DN_VW = DN_HEADS * DN_DV
DN_CONV_CH = 2 * DN_QK + DN_VW
MLA_Q_DIM = QK_NOPE + QK_ROPE
MLA_VW = MLA_HEADS * V_HEAD
MIX_WIDTH = DN_VW + MLA_VW
SPLIT_Z = DN_CONV_CH
SPLIT_BETA = SPLIT_Z + DN_VW
SPLIT_A = SPLIT_BETA + DN_HEADS
SPLIT_CQ = SPLIT_A + DN_HEADS
SPLIT_CKV = SPLIT_CQ + Q_LORA
SPLIT_KR = SPLIT_CKV + KV_LORA
N_IN = SPLIT_KR + QK_ROPE

DEEPNORM_ALPHA = (2.0 * DEPTH) ** 0.25
DEEPNORM_BETA = (8.0 * DEPTH) ** -0.25

kernel_name = "hybrid_gdn_mla_deepnorm_adaln"


def _layernorm(x, g, b, eps=1e-5):
    xf = x.astype(jnp.float32)
    mu = jnp.mean(xf, axis=-1, keepdims=True)
    var = jnp.mean(jnp.square(xf - mu), axis=-1, keepdims=True)
    y = (xf - mu) * lax.rsqrt(var + eps)
    return (y * g.astype(jnp.float32) + b.astype(jnp.float32)).astype(x.dtype)


def _rmsnorm(x, g, eps=1e-6):
    xf = x.astype(jnp.float32)
    y = xf * lax.rsqrt(jnp.mean(jnp.square(xf), axis=-1, keepdims=True) + eps)
    return (y * g.astype(jnp.float32)).astype(x.dtype)


def _l2norm(x, eps=1e-6):
    xf = x.astype(jnp.float32)
    return xf * lax.rsqrt(jnp.sum(jnp.square(xf), axis=-1, keepdims=True) + eps)


def _rope(t, cos, sin):
    t1, t2 = jnp.split(t.astype(jnp.float32), 2, axis=-1)
    return jnp.concatenate([t1 * cos - t2 * sin, t2 * cos + t1 * sin], axis=-1).astype(t.dtype)


def _gated_delta_rule(q, k, v, g, beta):
    b_, s_, h_, dk = q.shape
    dv = v.shape[-1]
    nc = s_ // CHUNK

    def to_chunks(t):
        return t.reshape(b_, nc, CHUNK, h_, -1).transpose(0, 3, 1, 2, 4)

    q, k, v = to_chunks(q), to_chunks(k), to_chunks(v)
    g = g.reshape(b_, nc, CHUNK, h_).transpose(0, 3, 1, 2)
    beta = beta.reshape(b_, nc, CHUNK, h_).transpose(0, 3, 1, 2)
    gc = jnp.cumsum(g, axis=-1)
    idx = jnp.arange(CHUNK)
    incl = idx[:, None] >= idx[None, :]
    strict = idx[:, None] > idx[None, :]
    diff = gc[..., :, None] - gc[..., None, :]
    decay = jnp.where(incl, jnp.exp(jnp.where(incl, diff, 0.0)), 0.0)
    kb = k * beta[..., None]
    a_mat = jnp.where(strict, jnp.einsum('bhnid,bhnjd->bhnij', kb, k) * decay, 0.0)
    lhs = a_mat + jnp.eye(CHUNK, dtype=a_mat.dtype)
    rhs = jnp.concatenate([kb * jnp.exp(gc)[..., None], v * beta[..., None]], axis=-1)
    wu = lax.linalg.triangular_solve(lhs, rhs, left_side=True, lower=True)
    w_c, u_c = wu[..., :dk], wu[..., dk:]
    attn = jnp.where(incl, jnp.einsum('bhnid,bhnjd->bhnij', q, k) * decay, 0.0)
    qg = q * jnp.exp(gc)[..., None]
    g_last = gc[..., -1]
    kd = k * jnp.exp(g_last[..., None] - gc)[..., None]

    def step(state, inp):
        qg_n, w_n, u_n, attn_n, kd_n, gl_n = inp
        v_new = u_n - jnp.einsum('bhcd,bhde->bhce', w_n, state)
        o = jnp.einsum('bhcd,bhde->bhce', qg_n, state) + jnp.einsum('bhij,bhje->bhie', attn_n, v_new)
        state = state * jnp.exp(gl_n)[..., None, None] + jnp.einsum('bhcd,bhce->bhde', kd_n, v_new)
        return state, o

    xs = (jnp.moveaxis(qg, 2, 0), jnp.moveaxis(w_c, 2, 0), jnp.moveaxis(u_c, 2, 0),
          jnp.moveaxis(attn, 2, 0), jnp.moveaxis(kd, 2, 0), jnp.moveaxis(g_last, 2, 0))
    state0 = jnp.zeros((b_, h_, dk, dv), jnp.float32)
    _, o = lax.scan(step, state0, xs)
    return o.transpose(1, 0, 3, 2, 4).reshape(b_, s_, h_, dv)


def _mla_attention(q_nope, q_rope, k_nope, k_rope, v):
    b_, s_, h_, dn = q_nope.shape
    dv = v.shape[-1]
    nq = s_ // Q_BLOCK
    scale = 1.0 / math.sqrt(QK_NOPE + QK_ROPE)
    qn_b = q_nope.reshape(b_, nq, Q_BLOCK, h_, dn).transpose(1, 0, 2, 3, 4)
    qr_b = q_rope.reshape(b_, nq, Q_BLOCK, h_, -1).transpose(1, 0, 2, 3, 4)
    starts = jnp.arange(nq, dtype=jnp.int32) * Q_BLOCK
    key_idx = jnp.arange(s_, dtype=jnp.int32)

    def block(args):
        qn, qr, start = args
        sc = (jnp.einsum('bqhd,bkhd->bhqk', qn, k_nope)
              + jnp.einsum('bqhd,bkd->bhqk', qr, k_rope)).astype(jnp.float32) * scale
        q_idx = start + jnp.arange(Q_BLOCK, dtype=jnp.int32)
        mask = key_idx[None, :] <= q_idx[:, None]
        p = jax.nn.softmax(jnp.where(mask, sc, -jnp.inf), axis=-1).astype(v.dtype)
        return jnp.einsum('bhqk,bkhd->bqhd', p, v)

    out = lax.map(block, (qn_b, qr_b, starts))
    return out.transpose(1, 0, 2, 3, 4).reshape(b_, s_, h_ * dv)


def _hybrid_mixer(h, cos, sin, w_in, conv_w, a_log, dt_bias, dn_norm_g,
                  q_norm_g, w_uq, kv_norm_g, w_ukv, w_o):
    b_, s_, _ = h.shape
    proj = h @ w_in
    qkv, z, b_raw, a_raw, cq, ckv, kr = jnp.split(
        proj, [SPLIT_Z, SPLIT_BETA, SPLIT_A, SPLIT_CQ, SPLIT_CKV, SPLIT_KR], axis=-1)

    qkv = lax.conv_general_dilated(qkv, conv_w, window_strides=(1,), padding=[(CONV_K - 1, 0)],
                                   dimension_numbers=('NWC', 'WIO', 'NWC'),
                                   feature_group_count=DN_CONV_CH)
    qkv = jax.nn.silu(qkv)
    q_dn, k_dn, v_dn = jnp.split(qkv, [DN_QK, 2 * DN_QK], axis=-1)
    q_dn = _l2norm(q_dn.reshape(b_, s_, DN_HEADS, DN_DK)) * (DN_DK ** -0.5)
    k_dn = _l2norm(k_dn.reshape(b_, s_, DN_HEADS, DN_DK))
    v_dn = v_dn.reshape(b_, s_, DN_HEADS, DN_DV).astype(jnp.float32)
    beta = jax.nn.sigmoid(b_raw.astype(jnp.float32))
    g = -jnp.exp(a_log.astype(jnp.float32)) * jax.nn.softplus(
        a_raw.astype(jnp.float32) + dt_bias.astype(jnp.float32))
    o_dn = _gated_delta_rule(q_dn, k_dn, v_dn, g, beta).astype(h.dtype)
    z = z.reshape(b_, s_, DN_HEADS, DN_DV)
    o_dn = (_rmsnorm(o_dn, dn_norm_g) * jax.nn.silu(z)).reshape(b_, s_, DN_VW)

    q_m = (_rmsnorm(cq, q_norm_g) @ w_uq).reshape(b_, s_, MLA_HEADS, MLA_Q_DIM)
    q_nope, q_rope = q_m[..., :QK_NOPE], q_m[..., QK_NOPE:]
    q_rope = _rope(q_rope, cos[:, :, None, :], sin[:, :, None, :])
    kv = (_rmsnorm(ckv, kv_norm_g) @ w_ukv).reshape(b_, s_, MLA_HEADS, QK_NOPE + V_HEAD)
    k_nope, v_m = kv[..., :QK_NOPE], kv[..., QK_NOPE:]
    k_rope = _rope(kr, cos, sin)
    o_mla = _mla_attention(q_nope, q_rope, k_nope, k_rope, v_m)

    return jnp.concatenate([o_dn, o_mla], axis=-1) @ w_o


def setup_inputs(seed: int = 0) -> dict:
    key = jax.random.key(seed)
    ks = jax.random.split(key, 24)
    f32 = jnp.float32
    nrm = lambda k, shape, s: jax.random.normal(k, shape, f32) * s
    x = jax.random.normal(ks[0], (BATCH, SEQ, D_MODEL), f32)
    c = jax.random.normal(ks[1], (BATCH, D_MODEL), f32)
    positions = (jnp.arange(SEQ, dtype=jnp.int32)[None, :]
                 + jax.random.randint(ks[2], (BATCH, 1), 0, SEQ, dtype=jnp.int32))
    w_ada = nrm(ks[3], (DEPTH, D_MODEL, 6 * D_MODEL), D_MODEL ** -0.5)
    b_ada = nrm(ks[4], (DEPTH, 6 * D_MODEL), 0.02)
    w_in = nrm(ks[5], (DEPTH, D_MODEL, N_IN), D_MODEL ** -0.5)
    conv_w = nrm(ks[6], (DEPTH, CONV_K, 1, DN_CONV_CH), CONV_K ** -0.5)
    a_log = jnp.log(jax.random.uniform(ks[7], (DEPTH, DN_HEADS), f32, 1.0, 16.0))
    dt = jnp.exp(jax.random.uniform(ks[8], (DEPTH, DN_HEADS), f32, math.log(1e-3), math.log(1e-1)))
    dt_bias = dt + jnp.log(-jnp.expm1(-dt))
    dn_norm_g = 1.0 + nrm(ks[9], (DEPTH, DN_DV), 0.02)
    q_norm_g = 1.0 + nrm(ks[10], (DEPTH, Q_LORA), 0.02)
    w_uq = nrm(ks[11], (DEPTH, Q_LORA, MLA_HEADS * MLA_Q_DIM), Q_LORA ** -0.5)
    kv_norm_g = 1.0 + nrm(ks[12], (DEPTH, KV_LORA), 0.02)
    w_ukv = nrm(ks[13], (DEPTH, KV_LORA, MLA_HEADS * (QK_NOPE + V_HEAD)), KV_LORA ** -0.5)
    w_o = nrm(ks[14], (DEPTH, MIX_WIDTH, D_MODEL), MIX_WIDTH ** -0.5 * DEEPNORM_BETA)
    ln1_g = 1.0 + nrm(ks[15], (DEPTH, D_MODEL), 0.02)
    ln1_b = nrm(ks[16], (DEPTH, D_MODEL), 0.02)
    w_gate = nrm(ks[17], (DEPTH, D_MODEL, D_FF), D_MODEL ** -0.5)
    w_up = nrm(ks[18], (DEPTH, D_MODEL, D_FF), D_MODEL ** -0.5)
    w_down = nrm(ks[19], (DEPTH, D_FF, D_MODEL), D_FF ** -0.5 * DEEPNORM_BETA)
    ln2_g = 1.0 + nrm(ks[20], (DEPTH, D_MODEL), 0.02)
    ln2_b = nrm(ks[21], (DEPTH, D_MODEL), 0.02)
    return {"x": x, "c": c, "positions": positions, "w_ada": w_ada, "b_ada": b_ada,
            "w_in": w_in, "conv_w": conv_w, "a_log": a_log, "dt_bias": dt_bias,
            "dn_norm_g": dn_norm_g, "q_norm_g": q_norm_g, "w_uq": w_uq,
            "kv_norm_g": kv_norm_g, "w_ukv": w_ukv, "w_o": w_o,
            "ln1_g": ln1_g, "ln1_b": ln1_b, "w_gate": w_gate, "w_up": w_up,
            "w_down": w_down, "ln2_g": ln2_g, "ln2_b": ln2_b}


def reference(x, c, positions, w_ada, b_ada, w_in, conv_w, a_log, dt_bias, dn_norm_g,
              q_norm_g, w_uq, kv_norm_g, w_ukv, w_o, ln1_g, ln1_b, w_gate, w_up,
              w_down, ln2_g, ln2_b):
    inv_freq = 1.0 / (ROPE_THETA ** (jnp.arange(0, QK_ROPE, 2, dtype=jnp.float32) / QK_ROPE))
    ang = positions.astype(jnp.float32)[..., None] * inv_freq
    cos, sin = jnp.cos(ang), jnp.sin(ang)
    c_act = jax.nn.silu(c)
    for l in range(DEPTH):
        mod = (c_act @ w_ada[l] + b_ada[l])[:, None, :]
        sh_m, sc_m, gt_m, sh_f, sc_f, gt_f = jnp.split(mod, 6, axis=-1)
        h = x * (1.0 + sc_m) + sh_m
        mix = _hybrid_mixer(h, cos, sin, w_in[l], conv_w[l], a_log[l], dt_bias[l], dn_norm_g[l],
                            q_norm_g[l], w_uq[l], kv_norm_g[l], w_ukv[l], w_o[l])
        x = _layernorm(DEEPNORM_ALPHA * x + gt_m * mix, ln1_g[l], ln1_b[l])
        h = x * (1.0 + sc_f) + sh_f
        ff = (jax.nn.silu(h @ w_gate[l]) * (h @ w_up[l])) @ w_down[l]
        x = _layernorm(DEEPNORM_ALPHA * x + gt_f * ff, ln2_g[l], ln2_b[l])
    return x
```

```python
import functools
import math

import jax
import jax.numpy as jnp
from jax import lax
from jax.experimental import pallas as pl
from jax.experimental.pallas import tpu as pltpu

F32 = jnp.float32
BF16 = jnp.bfloat16

DN_HEADS = 4
DN_DK = 128
DN_DV = 128
CONV_K = 4
MLA_HEADS = 4
QK_NOPE = 128
QK_ROPE = 64
V_HEAD = 128
Q_LORA = 512
KV_LORA = 256
ROPE_THETA = 10000.0

DN_QK = DN_HEADS * DN_DK
DN_VW = DN_HEADS * DN_DV
DN_CONV_CH = 2 * DN_QK + DN_VW

LANES = 128
SUBLANES = 8
GDN_CHUNK = 64
VMEM_LIMIT = 56 * 1024 * 1024

HALF_ROPE = QK_ROPE // 2
KR1_LANE = 0
KR2_LANE = 64
BETA_LANE = 96
A_LANE = 100
COL_QKV = 0
COL_Z = DN_CONV_CH
COL_CQ = COL_Z + DN_VW
COL_CKV = COL_CQ + Q_LORA
COL_MISC = COL_CKV + KV_LORA
N_IN_PAD = COL_MISC + LANES
QATT_W = 2 * LANES


def _dot(a, b):
    return jnp.dot(a, b, preferred_element_type=F32)


def _dot_f32(a, b):
    return jnp.dot(a, b, preferred_element_type=F32, precision=lax.Precision.HIGHEST)


def _dot_nt(a, b):
    return lax.dot_general(a, b, (((1,), (1,)), ((), ())), preferred_element_type=F32)


def _sigmoid(x):
    return 1.0 / (1.0 + jnp.exp(-x))


def _silu(x):
    return x * _sigmoid(x)


def _const_spec(shape):
    zeros = (0,) * len(shape)
    return pl.BlockSpec(shape, lambda *_: zeros, pipeline_mode=pl.Buffered(1))


def _mod_kernel(c_ref, w_ref, b_ref, o_ref):
    o_ref[...] = _dot(_silu(c_ref[...]), w_ref[...]) + b_ref[...]


def _modulation(c, w_ada, b_ada):
    bsz, d = c.shape
    n = w_ada.shape[1]
    return pl.pallas_call(
        _mod_kernel,
        grid=(n // d,),
        in_specs=[pl.BlockSpec((bsz, d), lambda j: (0, 0)),
                  pl.BlockSpec((d, d), lambda j: (0, j)),
                  pl.BlockSpec((1, d), lambda j: (0, j))],
        out_specs=pl.BlockSpec((bsz, d), lambda j: (0, j)),
        out_shape=jax.ShapeDtypeStruct((bsz, n), F32),
        compiler_params=pltpu.CompilerParams(dimension_semantics=("arbitrary",),
                                             vmem_limit_bytes=VMEM_LIMIT),
        name="modulation",
    )(c, w_ada, b_ada.reshape(1, n))


def _inproj_kernel(x_ref, mod_ref, pos_ref, win_ref, convw_ref, lane_ref, qng_ref, kvng_ref,
                   wuq_ref, wukv_ref,
                   qdn_ref, kdn_ref, vdn_ref, zs_ref, gb_ref, qatt_ref, katt_ref, vt_ref,
                   halo_ref, *, tm):
    i = pl.program_id(1)
    x = x_ref[0]
    sh_m = mod_ref[0, 0:1, :]
    sc_m = mod_ref[0, 1:2, :]
    h = (x * (1.0 + sc_m) + sh_m).astype(BF16)

    @pl.when(i == 0)
    def _():
        halo_ref[0:SUBLANES, :] = jnp.zeros((SUBLANES, DN_CONV_CH), F32)

    halo_ref[SUBLANES:SUBLANES + tm, :] = _dot(h, win_ref[:, COL_QKV:COL_QKV + DN_CONV_CH])
    conv = halo_ref[SUBLANES:SUBLANES + tm, :] * convw_ref[CONV_K - 1:CONV_K, :]
    for d in range(1, CONV_K):
        conv = conv + halo_ref[SUBLANES - d:SUBLANES - d + tm, :] * convw_ref[CONV_K - 1 - d:CONV_K - d, :]
    halo_ref[0:SUBLANES, :] = halo_ref[tm:tm + SUBLANES, :]
    qkv = _silu(conv)
    for hd in range(DN_HEADS):
        lo = hd * DN_DK
        q = qkv[:, lo:lo + DN_DK]
        k = qkv[:, DN_QK + lo:DN_QK + lo + DN_DK]
        q = q * (lax.rsqrt(jnp.sum(q * q, axis=-1, keepdims=True) + 1e-6) * (DN_DK ** -0.5))
        k = k * lax.rsqrt(jnp.sum(k * k, axis=-1, keepdims=True) + 1e-6)
        qdn_ref[0, :, lo:lo + DN_DK] = q.astype(BF16)
        kdn_ref[0, :, lo:lo + DN_DK] = k.astype(BF16)
    vdn_ref[0] = qkv[:, 2 * DN_QK:].astype(BF16)

    zs_ref[0] = _silu(_dot(h, win_ref[:, COL_Z:COL_Z + DN_VW])).astype(BF16)

    inv_freq = lane_ref[0:1, :]
    cos_mask = lane_ref[1:2, :]
    sin_sign = lane_ref[2:3, :]
    dt_bias = lane_ref[3:4, :]
    neg_a = -jnp.exp(lane_ref[4:5, :])
    ang = pos_ref[0] * inv_freq
    cos_t = jnp.cos(ang) * cos_mask
    sin_t = jnp.sin(ang) * sin_sign

    def rope(t):
        return t * cos_t + pltpu.roll(t, KR2_LANE - KR1_LANE, 1) * sin_t

    misc = _dot(h, win_ref[:, COL_MISC:COL_MISC + LANES])
    k_rope = rope(misc).astype(BF16)
    lane = lax.broadcasted_iota(jnp.int32, misc.shape, 1)
    a_in = misc + dt_bias
    softplus = jnp.maximum(a_in, 0.0) + jnp.log(1.0 + jnp.exp(-jnp.abs(a_in)))
    gb = jnp.where((lane >= BETA_LANE) & (lane < BETA_LANE + DN_HEADS), _sigmoid(misc),
                   jnp.where((lane >= A_LANE) & (lane < A_LANE + DN_HEADS), neg_a * softplus, 0.0))
    gb_ref[0] = gb

    cq = _dot(h, win_ref[:, COL_CQ:COL_CQ + Q_LORA])
    cq = cq * lax.rsqrt(jnp.mean(cq * cq, axis=-1, keepdims=True) + 1e-6) * qng_ref[...]
    qm = _dot(cq.astype(BF16), wuq_ref[...])
    scale = 1.0 / math.sqrt(QK_NOPE + QK_ROPE)
    for hd in range(MLA_HEADS):
        lo = hd * QATT_W
        qatt_ref[0, :, lo:lo + LANES] = (qm[:, lo:lo + LANES] * scale).astype(BF16)
        qatt_ref[0, :, lo + LANES:lo + QATT_W] = (rope(qm[:, lo + LANES:lo + QATT_W]) * scale).astype(BF16)

    ckv = _dot(h, win_ref[:, COL_CKV:COL_CKV + KV_LORA])
    ckv = ckv * lax.rsqrt(jnp.mean(ckv * ckv, axis=-1, keepdims=True) + 1e-6) * kvng_ref[...]
    kv = _dot(ckv.astype(BF16), wukv_ref[...])
    for hd in range(MLA_HEADS):
        lo = hd * QATT_W
        katt_ref[0, :, lo:lo + LANES] = kv[:, hd * QK_NOPE:(hd + 1) * QK_NOPE].astype(BF16)
        katt_ref[0, :, lo + LANES:lo + QATT_W] = k_rope
        v = kv[:, MLA_HEADS * QK_NOPE + hd * V_HEAD:MLA_HEADS * QK_NOPE + (hd + 1) * V_HEAD]
        vt_ref[0, hd, 0] = v.T.astype(BF16)


def _inproj(x, mod, posb, win_p, convw, lane_tab, qng, kvng, wuq_p, wukv_p, *, tm):
    bsz, s, d = x.shape
    nblk = s // tm
    row = lambda w: pl.BlockSpec((1, tm, w), lambda b, i: (b, i, 0))
    out_shape = (
        jax.ShapeDtypeStruct((bsz, s, DN_QK), BF16),
        jax.ShapeDtypeStruct((bsz, s, DN_QK), BF16),
        jax.ShapeDtypeStruct((bsz, s, DN_VW), BF16),
        jax.ShapeDtypeStruct((bsz, s, DN_VW), BF16),
        jax.ShapeDtypeStruct((bsz, s, LANES), F32),
        jax.ShapeDtypeStruct((bsz, s, MLA_HEADS * QATT_W), BF16),
        jax.ShapeDtypeStruct((bsz, s, MLA_HEADS * QATT_W), BF16),
        jax.ShapeDtypeStruct((bsz, MLA_HEADS, nblk, V_HEAD, tm), BF16),
    )
    out_specs = (row(DN_QK), row(DN_QK), row(DN_VW), row(DN_VW), row(LANES),
                 row(MLA_HEADS * QATT_W), row(MLA_HEADS * QATT_W),
                 pl.BlockSpec((1, MLA_HEADS, 1, V_HEAD, tm), lambda b, i: (b, 0, i, 0, 0)))
    return pl.pallas_call(
        functools.partial(_inproj_kernel, tm=tm),
        grid=(bsz, nblk),
        in_specs=[row(d),
                  pl.BlockSpec((1, mod.shape[1], d), lambda b, i: (b, 0, 0)),
                  row(LANES),
                  _const_spec(win_p.shape), _const_spec(convw.shape), _const_spec(lane_tab.shape),
                  _const_spec(qng.shape), _const_spec(kvng.shape),
                  _const_spec(wuq_p.shape), _const_spec(wukv_p.shape)],
        out_specs=out_specs,
        out_shape=out_shape,
        scratch_shapes=[pltpu.VMEM((tm + 2 * SUBLANES, DN_CONV_CH), F32)],
        compiler_params=pltpu.CompilerParams(dimension_semantics=("arbitrary", "arbitrary"),
                                             vmem_limit_bytes=VMEM_LIMIT),
        name="inproj",
    )(x, mod, posb, win_p, convw, lane_tab, qng, kvng, wuq_p, wukv_p)


def _gdn_kernel(q_ref, k_ref, v_ref, gb_ref, zs_ref, gn_ref, o_ref, state_ref, *, n_chunks):
    c = GDN_CHUNK

    @pl.when(pl.program_id(1) == 0)
    def _():
        state_ref[...] = jnp.zeros(state_ref.shape, F32)

    row = lax.broadcasted_iota(jnp.int32, (c, c), 0)
    col = lax.broadcasted_iota(jnp.int32, (c, c), 1)
    incl = row >= col
    strict = row > col
    tri = incl.astype(BF16)
    eye = (row == col).astype(F32)
    gnorm = gn_ref[...]

    def chunk(ci, carry):
        r0 = pl.multiple_of(ci * c, c)
        gbt = gb_ref[0, pl.ds(r0, c), :]
        hi = gbt.astype(BF16)
        r1 = gbt - hi.astype(F32)
        mid = r1.astype(BF16)
        lo = (r1 - mid.astype(F32)).astype(BF16)
        gc_all = _dot(tri, hi) + _dot(tri, mid) + _dot(tri, lo)
        gc_t = gc_all.T
        for hd in range(DN_HEADS):
            ls = slice(hd * DN_DK, (hd + 1) * DN_DK)
            gcol = gc_all[:, A_LANE + hd:A_LANE + hd + 1]
            grow = gc_t[A_LANE + hd:A_LANE + hd + 1, :]
            beta = gbt[:, BETA_LANE + hd:BETA_LANE + hd + 1]
            glast = gc_all[c - 1:c, A_LANE + hd:A_LANE + hd + 1]
            decay = jnp.where(incl, jnp.exp(jnp.where(incl, gcol - grow, 0.0)), 0.0)
            kb = k_ref[0, pl.ds(r0, c), ls]
            qb = q_ref[0, pl.ds(r0, c), ls]
            kf = kb.astype(F32)
            qf = qb.astype(F32)
            vf = v_ref[0, pl.ds(r0, c), ls].astype(F32)
            kq = _dot_nt(jnp.concatenate([kb, qb], axis=0), kb)
            a_mat = jnp.where(strict, kq[:c] * beta * decay, 0.0)
            attn = kq[c:] * decay
            p = -a_mat
            t = eye + p
            p = _dot_f32(p, p)
            for _ in range(int(math.log2(c)) - 2):
                tp = _dot_f32(jnp.concatenate([t, p], axis=0), p)
                t = t + tp[:c]
                p = tp[c:]
            t = t + _dot_f32(t, p)
            egc = jnp.exp(gcol)
            rhs = jnp.concatenate([(kf * (beta * egc)).astype(BF16), (vf * beta).astype(BF16)], axis=1)
            wu = _dot(t.astype(BF16), rhs)
            w = wu[:, :DN_DK]
            u = wu[:, DN_DK:]
            qg = (qf * egc).astype(BF16)
            kd_t = (kf * jnp.exp(glast - gcol)).T.astype(BF16)
            state = state_ref[hd]
            ws = _dot(jnp.concatenate([w.astype(BF16), qg], axis=0), state.astype(BF16))
            v_new = (u - ws[:c]).astype(BF16)
            o = ws[c:] + _dot(attn.astype(BF16), v_new)
            state_ref[hd] = state * jnp.exp(glast) + _dot(kd_t, v_new)
            o = o * lax.rsqrt(jnp.mean(o * o, axis=-1, keepdims=True) + 1e-6) * gnorm
            o_ref[0, pl.ds(r0, c), ls] = (o * zs_ref[0, pl.ds(r0, c), ls].astype(F32)).astype(BF16)
        return carry

    lax.fori_loop(0, n_chunks, chunk, 0)


def _gdn(q_dn, k_dn, v_dn, gb, zs, gnorm, *, tg):
    bsz, s, _ = q_dn.shape
    row = lambda w: pl.BlockSpec((1, tg, w), lambda b, i: (b, i, 0))
    return pl.pallas_call(
        functools.partial(_gdn_kernel, n_chunks=tg // GDN_CHUNK),
        grid=(bsz, s // tg),
        in_specs=[row(DN_QK), row(DN_QK), row(DN_VW), row(LANES), row(DN_VW), _const_spec(gnorm.shape)],
        out_specs=row(DN_VW),
        out_shape=jax.ShapeDtypeStruct((bsz, s, DN_VW), BF16),
        scratch_shapes=[pltpu.VMEM((DN_HEADS, DN_DK, DN_DV), F32)],
        compiler_params=pltpu.CompilerParams(dimension_semantics=("arbitrary", "arbitrary"),
                                             vmem_limit_bytes=VMEM_LIMIT),
        name="gdn",
    )(q_dn, k_dn, v_dn, gb, zs, gnorm)


NEG_BIG = -1e30


def _attn_kernel(q_ref, k_ref, vt_ref, o_ref, m_ref, l_ref, acc_ref, *, tq):
    qi = pl.program_id(2)
    q = q_ref[0]
    m_ref[...] = jnp.full(m_ref.shape, NEG_BIG, F32)
    l_ref[...] = jnp.zeros(l_ref.shape, F32)
    acc_ref[...] = jnp.zeros(acc_ref.shape, F32)

    def block(j, masked):
        r0 = pl.multiple_of(j * tq, tq)
        s = _dot_nt(k_ref[0, pl.ds(r0, tq), :], q)
        if masked:
            kidx = lax.broadcasted_iota(jnp.int32, s.shape, 0)
            qidx = lax.broadcasted_iota(jnp.int32, s.shape, 1)
            s = jnp.where(kidx <= qidx, s, NEG_BIG)
        m_old = m_ref[...]
        m_new = jnp.maximum(m_old, jnp.max(s, axis=0, keepdims=True))
        alpha = jnp.exp(m_old - m_new)
        p = jnp.exp(s - m_new)
        l_ref[...] = l_ref[...] * alpha + jnp.sum(p, axis=0, keepdims=True)
        acc_ref[...] = acc_ref[...] * alpha + _dot(vt_ref[0, 0, j], p.astype(BF16))
        m_ref[...] = m_new

    def body(j, carry):
        block(j, False)
        return carry

    lax.fori_loop(0, qi, body, 0)
    block(qi, True)
    o_ref[0] = (acc_ref[...] / l_ref[...]).T.astype(BF16)


def _attention(q_att, k_att, vt, *, tq):
    bsz, s, _ = q_att.shape
    nblk = s // tq
    return pl.pallas_call(
        functools.partial(_attn_kernel, tq=tq),
        grid=(bsz, MLA_HEADS, nblk),
        in_specs=[pl.BlockSpec((1, tq, QATT_W), lambda b, h, i: (b, i, h)),
                  pl.BlockSpec((1, s, QATT_W), lambda b, h, i: (b, 0, h)),
                  pl.BlockSpec((1, 1, nblk, V_HEAD, tq), lambda b, h, i: (b, h, 0, 0, 0))],
        out_specs=pl.BlockSpec((1, tq, V_HEAD), lambda b, h, i: (b, i, h)),
        out_shape=jax.ShapeDtypeStruct((bsz, s, MLA_HEADS * V_HEAD), BF16),
        scratch_shapes=[pltpu.VMEM((1, tq), F32), pltpu.VMEM((1, tq), F32), pltpu.VMEM((V_HEAD, tq), F32)],
        compiler_params=pltpu.CompilerParams(dimension_semantics=("arbitrary", "arbitrary", "arbitrary"),
                                             vmem_limit_bytes=VMEM_LIMIT),
        name="mla_attention",
    )(q_att, k_att, vt)


def _layernorm(y, g, b):
    mu = jnp.mean(y, axis=-1, keepdims=True)
    yc = y - mu
    var = jnp.mean(yc * yc, axis=-1, keepdims=True)
    return yc * lax.rsqrt(var + 1e-5) * g + b


def _out_kernel(x_ref, og_ref, om_ref, mod_ref, wo_ref, ln_ref, wg_ref, wu_ref, wd_ref, o_ref, act_ref,
                *, alpha, ff_chunk):
    x = x_ref[0]
    gt_m = mod_ref[0, 2:3, :]
    sh_f = mod_ref[0, 3:4, :]
    sc_f = mod_ref[0, 4:5, :]
    gt_f = mod_ref[0, 5:6, :]
    nw = og_ref.shape[-1]
    mix = _dot(og_ref[0], wo_ref[0:nw, :]) + _dot(om_ref[0], wo_ref[nw:, :])
    x1 = _layernorm(alpha * x + gt_m * mix, ln_ref[0:1, :], ln_ref[1:2, :])
    h = (x1 * (1.0 + sc_f) + sh_f).astype(BF16)
    d_ff = wg_ref.shape[1]
    for lo in range(0, d_ff, ff_chunk):
        g = _dot(h, wg_ref[:, lo:lo + ff_chunk])
        u = _dot(h, wu_ref[:, lo:lo + ff_chunk])
        act_ref[:, lo:lo + ff_chunk] = (_silu(g) * u).astype(BF16)
    ff = _dot(act_ref[...], wd_ref[...])
    o_ref[0] = _layernorm(alpha * x1 + gt_f * ff, ln_ref[2:3, :], ln_ref[3:4, :])


def _out_ffn(x, og, om, mod, wo, ln, wg, wu, wd, *, tm, alpha, ff_chunk):
    bsz, s, d = x.shape
    d_ff = wg.shape[1]
    row = lambda w: pl.BlockSpec((1, tm, w), lambda b, i: (b, i, 0))
    return pl.pallas_call(
        functools.partial(_out_kernel, alpha=alpha, ff_chunk=ff_chunk),
        grid=(bsz, s // tm),
        in_specs=[row(d), row(og.shape[-1]), row(om.shape[-1]),
                  pl.BlockSpec((1, mod.shape[1], d), lambda b, i: (b, 0, 0)),
                  _const_spec(wo.shape), _const_spec(ln.shape),
                  _const_spec(wg.shape), _const_spec(wu.shape), _const_spec(wd.shape)],
        out_specs=row(d),
        out_shape=jax.ShapeDtypeStruct((bsz, s, d), F32),
        scratch_shapes=[pltpu.VMEM((tm, d_ff), BF16)],
        compiler_params=pltpu.CompilerParams(dimension_semantics=("arbitrary", "arbitrary"),
                                             vmem_limit_bytes=VMEM_LIMIT),
        name="out_ffn",
    )(x, og, om, mod, wo, ln, wg, wu, wd)


def _pack_w_in(w_in):
    d = w_in.shape[0]
    split_z = DN_CONV_CH
    split_beta = split_z + DN_VW
    split_a = split_beta + DN_HEADS
    split_cq = split_a + DN_HEADS
    split_ckv = split_cq + Q_LORA
    split_kr = split_ckv + KV_LORA
    kr = w_in[:, split_kr:split_kr + QK_ROPE]
    zeros = lambda n: jnp.zeros((d, n), w_in.dtype)
    misc = jnp.concatenate([
        kr[:, :HALF_ROPE], zeros(KR2_LANE - HALF_ROPE),
        kr[:, HALF_ROPE:], zeros(BETA_LANE - KR2_LANE - HALF_ROPE),
        w_in[:, split_beta:split_a], w_in[:, split_a:split_cq],
        zeros(LANES - A_LANE - DN_HEADS)], axis=1)
    return jnp.concatenate([w_in[:, :split_beta], w_in[:, split_cq:split_kr], misc], axis=1).astype(BF16)


def _pack_w_uq(w_uq):
    q_lora = w_uq.shape[0]
    w = w_uq.reshape(q_lora, MLA_HEADS, QK_NOPE + QK_ROPE)
    zeros = lambda n: jnp.zeros((q_lora, MLA_HEADS, n), w_uq.dtype)
    tile = jnp.concatenate([
        w[:, :, :QK_NOPE],
        w[:, :, QK_NOPE:QK_NOPE + HALF_ROPE], zeros(KR2_LANE - HALF_ROPE),
        w[:, :, QK_NOPE + HALF_ROPE:], zeros(LANES - KR2_LANE - HALF_ROPE)], axis=2)
    return tile.reshape(q_lora, MLA_HEADS * QATT_W).astype(BF16)


def _pack_w_ukv(w_ukv):
    kv_lora = w_ukv.shape[0]
    w = w_ukv.reshape(kv_lora, MLA_HEADS, QK_NOPE + V_HEAD)
    return jnp.concatenate([w[:, :, :QK_NOPE].reshape(kv_lora, -1),
                            w[:, :, QK_NOPE:].reshape(kv_lora, -1)], axis=1).astype(BF16)


def _lane_table(a_log, dt_bias):
    inv_freq = 1.0 / (ROPE_THETA ** (jnp.arange(0, QK_ROPE, 2, dtype=F32) / QK_ROPE))
    z = jnp.zeros((LANES,), F32)
    ones = jnp.ones((HALF_ROPE,), F32)
    put = lambda vec, lane: z.at[lane:lane + vec.shape[0]].set(vec)
    rows = [put(inv_freq, KR1_LANE) + put(inv_freq, KR2_LANE),
            put(ones, KR1_LANE) + put(ones, KR2_LANE),
            put(-ones, KR1_LANE) + put(ones, KR2_LANE),
            put(dt_bias.astype(F32), A_LANE),
            put(a_log.astype(F32), A_LANE)]
    rows += [z] * (SUBLANES - len(rows))
    return jnp.stack(rows)


def _layer(x, c_mod, posb, w_in, conv_w, a_log, dt_bias, dn_norm_g, q_norm_g, w_uq, kv_norm_g, w_ukv,
           w_o, ln1_g, ln1_b, w_gate, w_up, w_down, ln2_g, ln2_b, *, depth, tm, tg, tq, ff_chunk):
    alpha = (2.0 * depth) ** 0.25
    q_dn, k_dn, v_dn, zs, gb, q_att, k_att, vt = _inproj(
        x, c_mod, posb, _pack_w_in(w_in), conv_w.reshape(CONV_K, DN_CONV_CH).astype(F32),
        _lane_table(a_log, dt_bias), q_norm_g.reshape(1, -1).astype(F32), kv_norm_g.reshape(1, -1).astype(F32),
        _pack_w_uq(w_uq), _pack_w_ukv(w_ukv), tm=tq)
    og = _gdn(q_dn, k_dn, v_dn, gb, zs, dn_norm_g.reshape(1, -1).astype(F32), tg=tg)
    om = _attention(q_att, k_att, vt, tq=tq)
    ln = jnp.stack([ln1_g, ln1_b, ln2_g, ln2_b]).astype(F32)
    return _out_ffn(x, og, om, c_mod, w_o.astype(BF16), ln, w_gate.astype(BF16), w_up.astype(BF16),
                    w_down.astype(BF16), tm=tm, alpha=alpha, ff_chunk=ff_chunk)


def kernel(x, c, positions, w_ada, b_ada, w_in, conv_w, a_log, dt_bias, dn_norm_g, q_norm_g, w_uq, kv_norm_g, w_ukv, w_o, ln1_g, ln1_b, w_gate, w_up, w_down, ln2_g, ln2_b):
    bsz, s, d = x.shape
    depth = w_in.shape[0]
    tile = min(512, s)
    posb = jnp.broadcast_to(positions.astype(F32)[..., None], (bsz, s, LANES))
    for l in range(depth):
        mod = _modulation(c, w_ada[l], b_ada[l]).reshape(bsz, 6, d)
        x = _layer(x, mod, posb, w_in[l], conv_w[l], a_log[l], dt_bias[l], dn_norm_g[l], q_norm_g[l], w_uq[l],
                   kv_norm_g[l], w_ukv[l], w_o[l], ln1_g[l], ln1_b[l], w_gate[l], w_up[l], w_down[l],
                   ln2_g[l], ln2_b[l], depth=depth, tm=tile, tg=tile, tq=tile, ff_chunk=256)
    return x
```

```python
import functools
import math

import jax
import jax.numpy as jnp
from jax import lax
from jax.experimental import pallas as pl
from jax.experimental.pallas import tpu as pltpu

F32 = jnp.float32
BF16 = jnp.bfloat16

DN_HEADS = 4
DN_DK = 128
DN_DV = 128
CONV_K = 4
MLA_HEADS = 4
QK_NOPE = 128
QK_ROPE = 64
V_HEAD = 128
Q_LORA = 512
KV_LORA = 256
ROPE_THETA = 10000.0

DN_QK = DN_HEADS * DN_DK
DN_VW = DN_HEADS * DN_DV
DN_CONV_CH = 2 * DN_QK + DN_VW

LANES = 128
SUBLANES = 8
GDN_CHUNK = 64
GDN_BASE = 8
VMEM_LIMIT = 56 * 1024 * 1024

HALF_ROPE = QK_ROPE // 2
KR1_LANE = 0
KR2_LANE = 64
BETA_LANE = 96
A_LANE = 100
COL_QKV = 0
COL_Z = DN_CONV_CH
COL_CQ = COL_Z + DN_VW
COL_CKV = COL_CQ + Q_LORA
COL_MISC = COL_CKV + KV_LORA
N_IN_PAD = COL_MISC + LANES
QATT_W = 2 * LANES


def _dot(a, b):
    return jnp.dot(a, b, preferred_element_type=F32)


def _dot_f32(a, b):
    return jnp.dot(a, b, preferred_element_type=F32, precision=lax.Precision.HIGHEST)


def _dot_nt(a, b):
    return lax.dot_general(a, b, (((1,), (1,)), ((), ())), preferred_element_type=F32)


def _sigmoid(x):
    return 1.0 / (1.0 + jnp.exp(-x))


def _silu(x):
    return x * _sigmoid(x)


def _const_spec(shape):
    zeros = (0,) * len(shape)
    return pl.BlockSpec(shape, lambda *_: zeros, pipeline_mode=pl.Buffered(1))


def _mod_kernel(c_ref, w_ref, b_ref, o_ref):
    o_ref[...] = _dot(_silu(c_ref[...]), w_ref[...]) + b_ref[...]


def _modulation(c, w_ada, b_ada):
    bsz, d = c.shape
    n = w_ada.shape[1]
    return pl.pallas_call(
        _mod_kernel,
        grid=(n // d,),
        in_specs=[pl.BlockSpec((bsz, d), lambda j: (0, 0)),
                  pl.BlockSpec((d, d), lambda j: (0, j)),
                  pl.BlockSpec((1, d), lambda j: (0, j))],
        out_specs=pl.BlockSpec((bsz, d), lambda j: (0, j)),
        out_shape=jax.ShapeDtypeStruct((bsz, n), F32),
        compiler_params=pltpu.CompilerParams(dimension_semantics=("arbitrary",),
                                             vmem_limit_bytes=VMEM_LIMIT),
        name="modulation",
    )(c, w_ada, b_ada.reshape(1, n))


def _inproj_kernel(x_ref, mod_ref, pos_ref, win_ref, convw_ref, lane_ref, qng_ref, kvng_ref,
                   wuq_ref, wukv_ref,
                   qdn_ref, kdn_ref, vdn_ref, zs_ref, gb_ref, qatt_ref, katt_ref, vt_ref,
                   halo_ref, *, tm):
    i = pl.program_id(1)
    x = x_ref[0]
    sh_m = mod_ref[0, 0:1, :]
    sc_m = mod_ref[0, 1:2, :]
    h = (x * (1.0 + sc_m) + sh_m).astype(BF16)

    @pl.when(i == 0)
    def _():
        halo_ref[0:SUBLANES, :] = jnp.zeros((SUBLANES, DN_CONV_CH), F32)

    halo_ref[SUBLANES:SUBLANES + tm, :] = _dot(h, win_ref[:, COL_QKV:COL_QKV + DN_CONV_CH])
    conv = halo_ref[SUBLANES:SUBLANES + tm, :] * convw_ref[CONV_K - 1:CONV_K, :]
    for d in range(1, CONV_K):
        conv = conv + halo_ref[SUBLANES - d:SUBLANES - d + tm, :] * convw_ref[CONV_K - 1 - d:CONV_K - d, :]
    halo_ref[0:SUBLANES, :] = halo_ref[tm:tm + SUBLANES, :]
    qkv = _silu(conv)
    for hd in range(DN_HEADS):
        lo = hd * DN_DK
        q = qkv[:, lo:lo + DN_DK]
        k = qkv[:, DN_QK + lo:DN_QK + lo + DN_DK]
        q = q * (lax.rsqrt(jnp.sum(q * q, axis=-1, keepdims=True) + 1e-6) * (DN_DK ** -0.5))
        k = k * lax.rsqrt(jnp.sum(k * k, axis=-1, keepdims=True) + 1e-6)
        qdn_ref[0, :, lo:lo + DN_DK] = q.astype(BF16)
        kdn_ref[0, :, lo:lo + DN_DK] = k.astype(BF16)
    vdn_ref[0] = qkv[:, 2 * DN_QK:].astype(BF16)

    zs_ref[0] = _silu(_dot(h, win_ref[:, COL_Z:COL_Z + DN_VW])).astype(BF16)

    inv_freq = lane_ref[0:1, :]
    cos_mask = lane_ref[1:2, :]
    sin_sign = lane_ref[2:3, :]
    dt_bias = lane_ref[3:4, :]
    neg_a = -jnp.exp(lane_ref[4:5, :])
    ang = pos_ref[0] * inv_freq
    cos_t = jnp.cos(ang) * cos_mask
    sin_t = jnp.sin(ang) * sin_sign

    def rope(t):
        return t * cos_t + pltpu.roll(t, KR2_LANE - KR1_LANE, 1) * sin_t

    misc = _dot(h, win_ref[:, COL_MISC:COL_MISC + LANES])
    k_rope = rope(misc).astype(BF16)
    lane = lax.broadcasted_iota(jnp.int32, misc.shape, 1)
    a_in = misc + dt_bias
    softplus = jnp.maximum(a_in, 0.0) + jnp.log(1.0 + jnp.exp(-jnp.abs(a_in)))
    gb = jnp.where((lane >= BETA_LANE) & (lane < BETA_LANE + DN_HEADS), _sigmoid(misc),
                   jnp.where((lane >= A_LANE) & (lane < A_LANE + DN_HEADS), neg_a * softplus, 0.0))
    gb_ref[0] = gb

    cq = _dot(h, win_ref[:, COL_CQ:COL_CQ + Q_LORA])
    cq = cq * lax.rsqrt(jnp.mean(cq * cq, axis=-1, keepdims=True) + 1e-6) * qng_ref[...]
    qm = _dot(cq.astype(BF16), wuq_ref[...])
    scale = 1.0 / math.sqrt(QK_NOPE + QK_ROPE)
    for hd in range(MLA_HEADS):
        lo = hd * QATT_W
        qatt_ref[0, :, lo:lo + LANES] = (qm[:, lo:lo + LANES] * scale).astype(BF16)
        qatt_ref[0, :, lo + LANES:lo + QATT_W] = (rope(qm[:, lo + LANES:lo + QATT_W]) * scale).astype(BF16)

    ckv = _dot(h, win_ref[:, COL_CKV:COL_CKV + KV_LORA])
    ckv = ckv * lax.rsqrt(jnp.mean(ckv * ckv, axis=-1, keepdims=True) + 1e-6) * kvng_ref[...]
    kv = _dot(ckv.astype(BF16), wukv_ref[...])
    for hd in range(MLA_HEADS):
        lo = hd * QATT_W
        katt_ref[0, :, lo:lo + LANES] = kv[:, hd * QK_NOPE:(hd + 1) * QK_NOPE].astype(BF16)
        katt_ref[0, :, lo + LANES:lo + QATT_W] = k_rope
        v = kv[:, MLA_HEADS * QK_NOPE + hd * V_HEAD:MLA_HEADS * QK_NOPE + (hd + 1) * V_HEAD]
        vt_ref[0, hd, 0] = v.T.astype(BF16)


def _inproj(x, mod, posb, win_p, convw, lane_tab, qng, kvng, wuq_p, wukv_p, *, tm):
    bsz, s, d = x.shape
    nblk = s // tm
    row = lambda w: pl.BlockSpec((1, tm, w), lambda b, i: (b, i, 0))
    out_shape = (
        jax.ShapeDtypeStruct((bsz, s, DN_QK), BF16),
        jax.ShapeDtypeStruct((bsz, s, DN_QK), BF16),
        jax.ShapeDtypeStruct((bsz, s, DN_VW), BF16),
        jax.ShapeDtypeStruct((bsz, s, DN_VW), BF16),
        jax.ShapeDtypeStruct((bsz, s, LANES), F32),
        jax.ShapeDtypeStruct((bsz, s, MLA_HEADS * QATT_W), BF16),
        jax.ShapeDtypeStruct((bsz, s, MLA_HEADS * QATT_W), BF16),
        jax.ShapeDtypeStruct((bsz, MLA_HEADS, nblk, V_HEAD, tm), BF16),
    )
    out_specs = (row(DN_QK), row(DN_QK), row(DN_VW), row(DN_VW), row(LANES),
                 row(MLA_HEADS * QATT_W), row(MLA_HEADS * QATT_W),
                 pl.BlockSpec((1, MLA_HEADS, 1, V_HEAD, tm), lambda b, i: (b, 0, i, 0, 0)))
    return pl.pallas_call(
        functools.partial(_inproj_kernel, tm=tm),
        grid=(bsz, nblk),
        in_specs=[row(d),
                  pl.BlockSpec((1, mod.shape[1], d), lambda b, i: (b, 0, 0)),
                  row(LANES),
                  _const_spec(win_p.shape), _const_spec(convw.shape), _const_spec(lane_tab.shape),
                  _const_spec(qng.shape), _const_spec(kvng.shape),
                  _const_spec(wuq_p.shape), _const_spec(wukv_p.shape)],
        out_specs=out_specs,
        out_shape=out_shape,
        scratch_shapes=[pltpu.VMEM((tm + 2 * SUBLANES, DN_CONV_CH), F32)],
        compiler_params=pltpu.CompilerParams(dimension_semantics=("arbitrary", "arbitrary"),
                                             vmem_limit_bytes=VMEM_LIMIT),
        name="inproj",
    )(x, mod, posb, win_p, convw, lane_tab, qng, kvng, wuq_p, wukv_p)


def _gdn_kernel(q_ref, k_ref, v_ref, gb_ref, zs_ref, gn_ref, o_ref,
                state_ref, w_ref, u_ref, qg_ref, attn_ref, kdt_ref, egl_ref, *, n_chunks, group):
    c = GDN_CHUNK

    @pl.when(pl.program_id(1) == 0)
    def _():
        state_ref[...] = jnp.zeros(state_ref.shape, F32)

    row = lax.broadcasted_iota(jnp.int32, (c, c), 0)
    col = lax.broadcasted_iota(jnp.int32, (c, c), 1)
    incl = row >= col
    strict = row > col
    tri = incl.astype(BF16)
    eye = (row == col).astype(F32)
    gnorm = gn_ref[...]
    base_shift = int(math.log2(GDN_BASE))
    diag_mask = strict & ((row >> base_shift) == (col >> base_shift))
    merge_masks = []
    for sh in range(base_shift, int(math.log2(c))):
        merge_masks.append(((row >> sh) == (col >> sh) + 1) & (((row >> sh) & 1) == 1))
    heads = range(DN_HEADS)
    lanes = [slice(hd * DN_DK, (hd + 1) * DN_DK) for hd in heads]

    def prepare(gi, carry):
        sys = [(ch, hd) for ch in range(group) for hd in heads]
        r0s = [pl.multiple_of((gi * group + ch) * c, c) for ch in range(group)]
        gbts = [gb_ref[0, pl.ds(r0, c), :] for r0 in r0s]
        gcs, gcts = [], []
        for gbt in gbts:
            hi = gbt.astype(BF16)
            r1 = gbt - hi.astype(F32)
            mid = r1.astype(BF16)
            lo = (r1 - mid.astype(F32)).astype(BF16)
            gc_all = _dot(tri, hi) + _dot(tri, mid) + _dot(tri, lo)
            gcs.append(gc_all)
            gcts.append(gc_all.T)
        gcol = [gcs[ch][:, A_LANE + hd:A_LANE + hd + 1] for ch, hd in sys]
        grow = [gcts[ch][A_LANE + hd:A_LANE + hd + 1, :] for ch, hd in sys]
        beta = [gbts[ch][:, BETA_LANE + hd:BETA_LANE + hd + 1] for ch, hd in sys]
        glast = [gcs[ch][c - 1:c, A_LANE + hd:A_LANE + hd + 1] for ch, hd in sys]
        decay = [jnp.where(incl, jnp.exp(jnp.where(incl, a - b, 0.0)), 0.0) for a, b in zip(gcol, grow)]
        kb = [k_ref[0, pl.ds(r0s[ch], c), lanes[hd]] for ch, hd in sys]
        qb = [q_ref[0, pl.ds(r0s[ch], c), lanes[hd]] for ch, hd in sys]
        kq = [_dot_nt(jnp.concatenate([k, q], axis=0), k) for k, q in zip(kb, qb)]
        a_mat = [x[:c] * b * d for x, b, d in zip(kq, beta, decay)]
        attn = [(x[c:] * d).astype(BF16) for x, d in zip(kq, decay)]
        dmat = [jnp.where(diag_mask, a, 0.0) for a in a_mat]
        t = [eye - d for d in dmat]
        p = [(-d).astype(BF16) for d in dmat]
        p = [_dot(x, x) for x in p]
        for _ in range(base_shift - 2):
            pb = [x.astype(BF16) for x in p]
            tp = [_dot(jnp.concatenate([x.astype(BF16), y], axis=0), y) for x, y in zip(t, pb)]
            t = [x + y[:c] for x, y in zip(t, tp)]
            p = [y[c:] for y in tp]
        t = [x + _dot(x.astype(BF16), y.astype(BF16)) for x, y in zip(t, p)]
        for mask in merge_masks:
            tb = [x.astype(BF16) for x in t]
            lt = [_dot(jnp.where(mask, a, 0.0).astype(BF16), x) for a, x in zip(a_mat, tb)]
            t = [x - _dot(xb, y.astype(BF16)) for x, xb, y in zip(t, tb, lt)]
        egc = [jnp.exp(x) for x in gcol]
        kf = [x.astype(F32) for x in kb]
        vf = [v_ref[0, pl.ds(r0s[ch], c), lanes[hd]].astype(F32) for ch, hd in sys]
        rhs = [jnp.concatenate([(k * (b * e)).astype(BF16), (v * b).astype(BF16)], axis=1)
               for k, v, b, e in zip(kf, vf, beta, egc)]
        wu = [_dot(x.astype(BF16), r) for x, r in zip(t, rhs)]
        qg = [(q.astype(F32) * e).astype(BF16) for q, e in zip(qb, egc)]
        kdt = [(k * jnp.exp(gl - gc)).T.astype(BF16) for k, gl, gc in zip(kf, glast, gcol)]
        for i, (ch, hd) in enumerate(sys):
            rows = pl.ds(r0s[ch], c)
            w_ref[rows, lanes[hd]] = wu[i][:, :DN_DK].astype(BF16)
            u_ref[rows, lanes[hd]] = wu[i][:, DN_DK:]
            qg_ref[rows, lanes[hd]] = qg[i]
            attn_ref[hd, rows, :] = attn[i]
            kdt_ref[hd, pl.ds(pl.multiple_of((gi * group + ch) * DN_DK, DN_DK), DN_DK), :] = kdt[i]
            egl_ref[pl.ds((gi * group + ch) * DN_HEADS + hd, 1), :] = jnp.broadcast_to(jnp.exp(glast[i]), (1, LANES))
        return carry

    def recur(ci, carry):
        rows = pl.ds(pl.multiple_of(ci * c, c), c)
        krows = pl.ds(pl.multiple_of(ci * DN_DK, DN_DK), DN_DK)
        state = [state_ref[hd] for hd in heads]
        ws = [_dot(jnp.concatenate([w_ref[rows, lanes[hd]], qg_ref[rows, lanes[hd]]], axis=0),
                   state[hd].astype(BF16)) for hd in heads]
        v_new = [(u_ref[rows, lanes[hd]] - ws[hd][:c]).astype(BF16) for hd in heads]
        o = [ws[hd][c:] + _dot(attn_ref[hd, rows, :], v_new[hd]) for hd in heads]
        for hd in heads:
            egl = egl_ref[pl.ds(ci * DN_HEADS + hd, 1), :]
            state_ref[hd] = state[hd] * egl + _dot(kdt_ref[hd, krows, :], v_new[hd])
        for hd in heads:
            on = o[hd] * lax.rsqrt(jnp.mean(o[hd] * o[hd], axis=-1, keepdims=True) + 1e-6) * gnorm
            o_ref[0, rows, lanes[hd]] = (on * zs_ref[0, rows, lanes[hd]].astype(F32)).astype(BF16)
        return carry

    lax.fori_loop(0, n_chunks // group, prepare, 0)
    lax.fori_loop(0, n_chunks, recur, 0)


def _gdn(q_dn, k_dn, v_dn, gb, zs, gnorm, *, tg, group):
    bsz, s, _ = q_dn.shape
    n_chunks = tg // GDN_CHUNK
    row = lambda w: pl.BlockSpec((1, tg, w), lambda b, i: (b, i, 0))
    return pl.pallas_call(
        functools.partial(_gdn_kernel, n_chunks=n_chunks, group=group),
        grid=(bsz, s // tg),
        in_specs=[row(DN_QK), row(DN_QK), row(DN_VW), row(LANES), row(DN_VW), _const_spec(gnorm.shape)],
        out_specs=row(DN_VW),
        out_shape=jax.ShapeDtypeStruct((bsz, s, DN_VW), BF16),
        scratch_shapes=[pltpu.VMEM((DN_HEADS, DN_DK, DN_DV), F32),
                        pltpu.VMEM((tg, DN_QK), BF16),
                        pltpu.VMEM((tg, DN_VW), F32),
                        pltpu.VMEM((tg, DN_QK), BF16),
                        pltpu.VMEM((DN_HEADS, tg, GDN_CHUNK), BF16),
                        pltpu.VMEM((DN_HEADS, n_chunks * DN_DK, GDN_CHUNK), BF16),
                        pltpu.VMEM((n_chunks * DN_HEADS, LANES), F32)],
        compiler_params=pltpu.CompilerParams(dimension_semantics=("arbitrary", "arbitrary"),
                                             vmem_limit_bytes=VMEM_LIMIT),
        name="gdn",
    )(q_dn, k_dn, v_dn, gb, zs, gnorm)


NEG_BIG = -1e30


def _attn_kernel(q_ref, k_ref, vt_ref, o_ref, m_ref, l_ref, acc_ref, *, tq):
    qi = pl.program_id(2)
    q = q_ref[0]
    m_ref[...] = jnp.full(m_ref.shape, NEG_BIG, F32)
    l_ref[...] = jnp.zeros(l_ref.shape, F32)
    acc_ref[...] = jnp.zeros(acc_ref.shape, F32)

    def block(j, masked):
        r0 = pl.multiple_of(j * tq, tq)
        s = _dot_nt(k_ref[0, pl.ds(r0, tq), :], q)
        if masked:
            kidx = lax.broadcasted_iota(jnp.int32, s.shape, 0)
            qidx = lax.broadcasted_iota(jnp.int32, s.shape, 1)
            s = jnp.where(kidx <= qidx, s, NEG_BIG)
        m_old = m_ref[...]
        m_new = jnp.maximum(m_old, jnp.max(s, axis=0, keepdims=True))
        alpha = jnp.exp(m_old - m_new)
        p = jnp.exp(s - m_new)
        l_ref[...] = l_ref[...] * alpha + jnp.sum(p, axis=0, keepdims=True)
        acc_ref[...] = acc_ref[...] * alpha + _dot(vt_ref[0, 0, j], p.astype(BF16))
        m_ref[...] = m_new

    def body(j, carry):
        block(j, False)
        return carry

    lax.fori_loop(0, qi, body, 0)
    block(qi, True)
    o_ref[0] = (acc_ref[...] / l_ref[...]).T.astype(BF16)


def _attention(q_att, k_att, vt, *, tq):
    bsz, s, _ = q_att.shape
    nblk = s // tq
    return pl.pallas_call(
        functools.partial(_attn_kernel, tq=tq),
        grid=(bsz, MLA_HEADS, nblk),
        in_specs=[pl.BlockSpec((1, tq, QATT_W), lambda b, h, i: (b, i, h)),
                  pl.BlockSpec((1, s, QATT_W), lambda b, h, i: (b, 0, h)),
                  pl.BlockSpec((1, 1, nblk, V_HEAD, tq), lambda b, h, i: (b, h, 0, 0, 0))],
        out_specs=pl.BlockSpec((1, tq, V_HEAD), lambda b, h, i: (b, i, h)),
        out_shape=jax.ShapeDtypeStruct((bsz, s, MLA_HEADS * V_HEAD), BF16),
        scratch_shapes=[pltpu.VMEM((1, tq), F32), pltpu.VMEM((1, tq), F32), pltpu.VMEM((V_HEAD, tq), F32)],
        compiler_params=pltpu.CompilerParams(dimension_semantics=("arbitrary", "arbitrary", "arbitrary"),
                                             vmem_limit_bytes=VMEM_LIMIT),
        name="mla_attention",
    )(q_att, k_att, vt)


def _layernorm(y, g, b):
    mu = jnp.mean(y, axis=-1, keepdims=True)
    yc = y - mu
    var = jnp.mean(yc * yc, axis=-1, keepdims=True)
    return yc * lax.rsqrt(var + 1e-5) * g + b


def _out_kernel(x_ref, og_ref, om_ref, mod_ref, wo_ref, ln_ref, wg_ref, wu_ref, wd_ref, o_ref, act_ref,
                *, alpha, ff_chunk):
    x = x_ref[0]
    gt_m = mod_ref[0, 2:3, :]
    sh_f = mod_ref[0, 3:4, :]
    sc_f = mod_ref[0, 4:5, :]
    gt_f = mod_ref[0, 5:6, :]
    nw = og_ref.shape[-1]
    mix = _dot(og_ref[0], wo_ref[0:nw, :]) + _dot(om_ref[0], wo_ref[nw:, :])
    x1 = _layernorm(alpha * x + gt_m * mix, ln_ref[0:1, :], ln_ref[1:2, :])
    h = (x1 * (1.0 + sc_f) + sh_f).astype(BF16)
    d_ff = wg_ref.shape[1]
    for lo in range(0, d_ff, ff_chunk):
        g = _dot(h, wg_ref[:, lo:lo + ff_chunk])
        u = _dot(h, wu_ref[:, lo:lo + ff_chunk])
        act_ref[:, lo:lo + ff_chunk] = (_silu(g) * u).astype(BF16)
    ff = _dot(act_ref[...], wd_ref[...])
    o_ref[0] = _layernorm(alpha * x1 + gt_f * ff, ln_ref[2:3, :], ln_ref[3:4, :])


def _out_ffn(x, og, om, mod, wo, ln, wg, wu, wd, *, tm, alpha, ff_chunk):
    bsz, s, d = x.shape
    d_ff = wg.shape[1]
    row = lambda w: pl.BlockSpec((1, tm, w), lambda b, i: (b, i, 0))
    return pl.pallas_call(
        functools.partial(_out_kernel, alpha=alpha, ff_chunk=ff_chunk),
        grid=(bsz, s // tm),
        in_specs=[row(d), row(og.shape[-1]), row(om.shape[-1]),
                  pl.BlockSpec((1, mod.shape[1], d), lambda b, i: (b, 0, 0)),
                  _const_spec(wo.shape), _const_spec(ln.shape),
                  _const_spec(wg.shape), _const_spec(wu.shape), _const_spec(wd.shape)],
        out_specs=row(d),
        out_shape=jax.ShapeDtypeStruct((bsz, s, d), F32),
        scratch_shapes=[pltpu.VMEM((tm, d_ff), BF16)],
        compiler_params=pltpu.CompilerParams(dimension_semantics=("arbitrary", "arbitrary"),
                                             vmem_limit_bytes=VMEM_LIMIT),
        name="out_ffn",
    )(x, og, om, mod, wo, ln, wg, wu, wd)


def _pack_w_in(w_in):
    d = w_in.shape[0]
    split_z = DN_CONV_CH
    split_beta = split_z + DN_VW
    split_a = split_beta + DN_HEADS
    split_cq = split_a + DN_HEADS
    split_ckv = split_cq + Q_LORA
    split_kr = split_ckv + KV_LORA
    kr = w_in[:, split_kr:split_kr + QK_ROPE]
    zeros = lambda n: jnp.zeros((d, n), w_in.dtype)
    misc = jnp.concatenate([
        kr[:, :HALF_ROPE], zeros(KR2_LANE - HALF_ROPE),
        kr[:, HALF_ROPE:], zeros(BETA_LANE - KR2_LANE - HALF_ROPE),
        w_in[:, split_beta:split_a], w_in[:, split_a:split_cq],
        zeros(LANES - A_LANE - DN_HEADS)], axis=1)
    return jnp.concatenate([w_in[:, :split_beta], w_in[:, split_cq:split_kr], misc], axis=1).astype(BF16)


def _pack_w_uq(w_uq):
    q_lora = w_uq.shape[0]
    w = w_uq.reshape(q_lora, MLA_HEADS, QK_NOPE + QK_ROPE)
    zeros = lambda n: jnp.zeros((q_lora, MLA_HEADS, n), w_uq.dtype)
    tile = jnp.concatenate([
        w[:, :, :QK_NOPE],
        w[:, :, QK_NOPE:QK_NOPE + HALF_ROPE], zeros(KR2_LANE - HALF_ROPE),
        w[:, :, QK_NOPE + HALF_ROPE:], zeros(LANES - KR2_LANE - HALF_ROPE)], axis=2)
    return tile.reshape(q_lora, MLA_HEADS * QATT_W).astype(BF16)


def _pack_w_ukv(w_ukv):
    kv_lora = w_ukv.shape[0]
    w = w_ukv.reshape(kv_lora, MLA_HEADS, QK_NOPE + V_HEAD)
    return jnp.concatenate([w[:, :, :QK_NOPE].reshape(kv_lora, -1),
                            w[:, :, QK_NOPE:].reshape(kv_lora, -1)], axis=1).astype(BF16)


def _lane_table(a_log, dt_bias):
    inv_freq = 1.0 / (ROPE_THETA ** (jnp.arange(0, QK_ROPE, 2, dtype=F32) / QK_ROPE))
    z = jnp.zeros((LANES,), F32)
    ones = jnp.ones((HALF_ROPE,), F32)
    put = lambda vec, lane: z.at[lane:lane + vec.shape[0]].set(vec)
    rows = [put(inv_freq, KR1_LANE) + put(inv_freq, KR2_LANE),
            put(ones, KR1_LANE) + put(ones, KR2_LANE),
            put(-ones, KR1_LANE) + put(ones, KR2_LANE),
            put(dt_bias.astype(F32), A_LANE),
            put(a_log.astype(F32), A_LANE)]
    rows += [z] * (SUBLANES - len(rows))
    return jnp.stack(rows)


def _layer(x, c_mod, posb, w_in, conv_w, a_log, dt_bias, dn_norm_g, q_norm_g, w_uq, kv_norm_g, w_ukv,
           w_o, ln1_g, ln1_b, w_gate, w_up, w_down, ln2_g, ln2_b, *, depth, tm, tg, tq, ff_chunk):
    alpha = (2.0 * depth) ** 0.25
    q_dn, k_dn, v_dn, zs, gb, q_att, k_att, vt = _inproj(
        x, c_mod, posb, _pack_w_in(w_in), conv_w.reshape(CONV_K, DN_CONV_CH).astype(F32),
        _lane_table(a_log, dt_bias), q_norm_g.reshape(1, -1).astype(F32), kv_norm_g.reshape(1, -1).astype(F32),
        _pack_w_uq(w_uq), _pack_w_ukv(w_ukv), tm=tq)
    og = _gdn(q_dn, k_dn, v_dn, gb, zs, dn_norm_g.reshape(1, -1).astype(F32), tg=tg, group=2)
    om = _attention(q_att, k_att, vt, tq=tq)
    ln = jnp.stack([ln1_g, ln1_b, ln2_g, ln2_b]).astype(F32)
    return _out_ffn(x, og, om, c_mod, w_o.astype(BF16), ln, w_gate.astype(BF16), w_up.astype(BF16),
                    w_down.astype(BF16), tm=tm, alpha=alpha, ff_chunk=ff_chunk)


def kernel(x, c, positions, w_ada, b_ada, w_in, conv_w, a_log, dt_bias, dn_norm_g, q_norm_g, w_uq, kv_norm_g, w_ukv, w_o, ln1_g, ln1_b, w_gate, w_up, w_down, ln2_g, ln2_b):
    bsz, s, d = x.shape
    depth = w_in.shape[0]
    tile = min(512, s)
    posb = jnp.broadcast_to(positions.astype(F32)[..., None], (bsz, s, LANES))
    for l in range(depth):
        mod = _modulation(c, w_ada[l], b_ada[l]).reshape(bsz, 6, d)
        x = _layer(x, mod, posb, w_in[l], conv_w[l], a_log[l], dt_bias[l], dn_norm_g[l], q_norm_g[l], w_uq[l],
                   kv_norm_g[l], w_ukv[l], w_o[l], ln1_g[l], ln1_b[l], w_gate[l], w_up[l], w_down[l],
                   ln2_g[l], ln2_b[l], depth=depth, tm=tile, tg=tile, tq=tile, ff_chunk=256)
    return x
```

```python
import functools
import math

import jax
import jax.numpy as jnp
from jax import lax
from jax.experimental import pallas as pl
from jax.experimental.pallas import tpu as pltpu

F32 = jnp.float32
BF16 = jnp.bfloat16

DN_HEADS = 4
DN_DK = 128
DN_DV = 128
CONV_K = 4
MLA_HEADS = 4
QK_NOPE = 128
QK_ROPE = 64
V_HEAD = 128
Q_LORA = 512
KV_LORA = 256
ROPE_THETA = 10000.0

DN_QK = DN_HEADS * DN_DK
DN_VW = DN_HEADS * DN_DV
DN_CONV_CH = 2 * DN_QK + DN_VW

LANES = 128
SUBLANES = 8
MXU_COLS = 256
GDN_CHUNK = 64
GDN_BASE = 8
GDN_TILE = 1024
GDN_GROUP = 4
VMEM_LIMIT = 56 * 1024 * 1024

HALF_ROPE = QK_ROPE // 2
KR1_LANE = 0
KR2_LANE = 64
BETA_LANE = 96
A_LANE = 100
COL_QKV = 0
COL_Z = DN_CONV_CH
COL_CQ = COL_Z + DN_VW
COL_CKV = COL_CQ + Q_LORA
COL_MISC = COL_CKV + KV_LORA
N_IN_PAD = COL_MISC + LANES
QATT_W = 2 * LANES


def _dot(a, b):
    return jnp.dot(a, b, preferred_element_type=F32)


def _dot_f32(a, b):
    return jnp.dot(a, b, preferred_element_type=F32, precision=lax.Precision.HIGHEST)


def _dot_nt(a, b):
    return lax.dot_general(a, b, (((1,), (1,)), ((), ())), preferred_element_type=F32)


def _sigmoid(x):
    return 0.5 + 0.5 * jnp.tanh(0.5 * x)


def _silu(x):
    half = 0.5 * x
    return half + half * jnp.tanh(half)


def _aligned(start, multiple):
    return start if isinstance(start, int) else pl.multiple_of(start, multiple)


def _const_spec(shape):
    zeros = (0,) * len(shape)
    return pl.BlockSpec(shape, lambda *_: zeros, pipeline_mode=pl.Buffered(1))


def _mod_kernel(c_ref, w_ref, b_ref, o_ref):
    o_ref[...] = _dot(_silu(c_ref[...]), w_ref[...]) + b_ref[...]


def _modulation(c, w_ada, b_ada):
    bsz, d = c.shape
    n = w_ada.shape[1]
    return pl.pallas_call(
        _mod_kernel,
        grid=(n // d,),
        in_specs=[pl.BlockSpec((bsz, d), lambda j: (0, 0)),
                  pl.BlockSpec((d, d), lambda j: (0, j)),
                  pl.BlockSpec((1, d), lambda j: (0, j))],
        out_specs=pl.BlockSpec((bsz, d), lambda j: (0, j)),
        out_shape=jax.ShapeDtypeStruct((bsz, n), F32),
        compiler_params=pltpu.CompilerParams(dimension_semantics=("arbitrary",),
                                             vmem_limit_bytes=VMEM_LIMIT),
        name="modulation",
    )(c, w_ada, b_ada.reshape(1, n))


def _inproj_kernel(x_ref, mod_ref, rot_ref, win_ref, convw_ref, lane_ref, qng_ref, kvng_ref,
                   wuq_ref, wukv_ref,
                   qdn_ref, kdn_ref, vdn_ref, zs_ref, gb_ref, qatt_ref, katt_ref, vt_ref,
                   halo_ref, *, tm):
    i = pl.program_id(1)
    x = x_ref[0]
    sh_m = mod_ref[0, 0:1, :]
    sc_m = mod_ref[0, 1:2, :]
    h = (x * (1.0 + sc_m) + sh_m).astype(BF16)

    @pl.when(i == 0)
    def _():
        halo_ref[0:SUBLANES, :] = jnp.zeros((SUBLANES, DN_CONV_CH), F32)

    for lo in range(0, DN_CONV_CH, MXU_COLS):
        cols = slice(lo, lo + MXU_COLS)
        halo_ref[SUBLANES:SUBLANES + tm, cols] = _dot(h, win_ref[:, COL_QKV + lo:COL_QKV + lo + MXU_COLS])
        conv = halo_ref[SUBLANES:SUBLANES + tm, cols] * convw_ref[CONV_K - 1:CONV_K, cols]
        for d in range(1, CONV_K):
            conv = conv + halo_ref[SUBLANES - d:SUBLANES - d + tm, cols] * convw_ref[CONV_K - 1 - d:CONV_K - d, cols]
        act = _silu(conv)
        if lo >= 2 * DN_QK:
            vdn_ref[0, :, lo - 2 * DN_QK:lo - 2 * DN_QK + MXU_COLS] = act.astype(BF16)
            continue
        dst, off, gain = (qdn_ref, lo, DN_DK ** -0.5) if lo < DN_QK else (kdn_ref, lo - DN_QK, 1.0)
        for sub in range(0, MXU_COLS, DN_DK):
            t = act[:, sub:sub + DN_DK]
            t = t * (lax.rsqrt(jnp.sum(t * t, axis=-1, keepdims=True) + 1e-6) * gain)
            dst[0, :, off + sub:off + sub + DN_DK] = t.astype(BF16)
    halo_ref[0:SUBLANES, :] = halo_ref[tm:tm + SUBLANES, :]

    for lo in range(0, DN_VW, MXU_COLS):
        zs_ref[0, :, lo:lo + MXU_COLS] = _silu(_dot(h, win_ref[:, COL_Z + lo:COL_Z + lo + MXU_COLS])).astype(BF16)

    sin_sign = lane_ref[0:1, :]
    dt_bias = lane_ref[1:2, :]
    neg_a = -jnp.exp(lane_ref[2:3, :])
    cos_t = rot_ref[0, :, 0:LANES]
    sin_t = rot_ref[0, :, LANES:2 * LANES] * sin_sign

    def rope(t):
        return t * cos_t + pltpu.roll(t, KR2_LANE - KR1_LANE, 1) * sin_t

    misc = _dot(h, win_ref[:, COL_MISC:COL_MISC + LANES])
    k_rope = rope(misc).astype(BF16)
    lane = lax.broadcasted_iota(jnp.int32, misc.shape, 1)
    a_in = misc + dt_bias
    softplus = jnp.maximum(a_in, 0.0) + jnp.log(1.0 + jnp.exp(-jnp.abs(a_in)))
    gb = jnp.where((lane >= BETA_LANE) & (lane < BETA_LANE + DN_HEADS), _sigmoid(misc),
                   jnp.where((lane >= A_LANE) & (lane < A_LANE + DN_HEADS), neg_a * softplus, 0.0))
    gb_ref[0] = gb

    cq = _dot(h, win_ref[:, COL_CQ:COL_CQ + Q_LORA])
    cq = cq * lax.rsqrt(jnp.mean(cq * cq, axis=-1, keepdims=True) + 1e-6) * qng_ref[...]
    qm = _dot(cq.astype(BF16), wuq_ref[...])
    scale = math.log2(math.e) / math.sqrt(QK_NOPE + QK_ROPE)
    for hd in range(MLA_HEADS):
        lo = hd * QATT_W
        qatt_ref[0, :, lo:lo + LANES] = (qm[:, lo:lo + LANES] * scale).astype(BF16)
        qatt_ref[0, :, lo + LANES:lo + QATT_W] = (rope(qm[:, lo + LANES:lo + QATT_W]) * scale).astype(BF16)

    ckv = _dot(h, win_ref[:, COL_CKV:COL_CKV + KV_LORA])
    ckv = ckv * lax.rsqrt(jnp.mean(ckv * ckv, axis=-1, keepdims=True) + 1e-6) * kvng_ref[...]
    kv = _dot(ckv.astype(BF16), wukv_ref[...])
    for hd in range(MLA_HEADS):
        lo = hd * QATT_W
        katt_ref[0, :, lo:lo + LANES] = kv[:, hd * QK_NOPE:(hd + 1) * QK_NOPE].astype(BF16)
        katt_ref[0, :, lo + LANES:lo + QATT_W] = k_rope
        v = kv[:, MLA_HEADS * QK_NOPE + hd * V_HEAD:MLA_HEADS * QK_NOPE + (hd + 1) * V_HEAD]
        vt_ref[0, hd, 0] = v.T.astype(BF16)


def _inproj(x, mod, rot, win_p, convw, lane_tab, qng, kvng, wuq_p, wukv_p, *, tm):
    bsz, s, d = x.shape
    nblk = s // tm
    row = lambda w: pl.BlockSpec((1, tm, w), lambda b, i: (b, i, 0))
    out_shape = (
        jax.ShapeDtypeStruct((bsz, s, DN_QK), BF16),
        jax.ShapeDtypeStruct((bsz, s, DN_QK), BF16),
        jax.ShapeDtypeStruct((bsz, s, DN_VW), BF16),
        jax.ShapeDtypeStruct((bsz, s, DN_VW), BF16),
        jax.ShapeDtypeStruct((bsz, s, LANES), F32),
        jax.ShapeDtypeStruct((bsz, s, MLA_HEADS * QATT_W), BF16),
        jax.ShapeDtypeStruct((bsz, s, MLA_HEADS * QATT_W), BF16),
        jax.ShapeDtypeStruct((bsz, MLA_HEADS, nblk, V_HEAD, tm), BF16),
    )
    out_specs = (row(DN_QK), row(DN_QK), row(DN_VW), row(DN_VW), row(LANES),
                 row(MLA_HEADS * QATT_W), row(MLA_HEADS * QATT_W),
                 pl.BlockSpec((1, MLA_HEADS, 1, V_HEAD, tm), lambda b, i: (b, 0, i, 0, 0)))
    return pl.pallas_call(
        functools.partial(_inproj_kernel, tm=tm),
        grid=(bsz, nblk),
        in_specs=[row(d),
                  pl.BlockSpec((1, mod.shape[1], d), lambda b, i: (b, 0, 0)),
                  row(2 * LANES),
                  _const_spec(win_p.shape), _const_spec(convw.shape), _const_spec(lane_tab.shape),
                  _const_spec(qng.shape), _const_spec(kvng.shape),
                  _const_spec(wuq_p.shape), _const_spec(wukv_p.shape)],
        out_specs=out_specs,
        out_shape=out_shape,
        scratch_shapes=[pltpu.VMEM((tm + 2 * SUBLANES, DN_CONV_CH), F32)],
        compiler_params=pltpu.CompilerParams(dimension_semantics=("arbitrary", "arbitrary"),
                                             vmem_limit_bytes=VMEM_LIMIT),
        name="inproj",
    )(x, mod, rot, win_p, convw, lane_tab, qng, kvng, wuq_p, wukv_p)


def _gdn_kernel(q_ref, k_ref, v_ref, gb_ref, zs_ref, gn_ref, o_ref,
                state_ref, w_ref, u_ref, qg_ref, attn_ref, kdt_ref, egl_ref, *, n_chunks, group):
    c = GDN_CHUNK

    @pl.when(pl.program_id(1) == 0)
    def _():
        state_ref[...] = jnp.zeros(state_ref.shape, F32)

    row = lax.broadcasted_iota(jnp.int32, (c, c), 0)
    col = lax.broadcasted_iota(jnp.int32, (c, c), 1)
    incl = row >= col
    strict = row > col
    tri = incl.astype(BF16)
    eye = (row == col).astype(F32)
    gnorm = gn_ref[...]
    base_shift = int(math.log2(GDN_BASE))
    diag_mask = strict & ((row >> base_shift) == (col >> base_shift))
    merge_masks = []
    for sh in range(base_shift, int(math.log2(c))):
        merge_masks.append(((row >> sh) == (col >> sh) + 1) & (((row >> sh) & 1) == 1))
    heads = range(DN_HEADS)
    lanes = [slice(hd * DN_DK, (hd + 1) * DN_DK) for hd in heads]

    def prepare(gi, slot):
        sys = [(ch, hd) for ch in range(group) for hd in heads]
        r0s = [_aligned((gi * group + ch) * c, c) for ch in range(group)]
        gbts = [gb_ref[0, pl.ds(r0, c), :] for r0 in r0s]
        gcs, gcts = [], []
        for gbt in gbts:
            hi = gbt.astype(BF16)
            r1 = gbt - hi.astype(F32)
            mid = r1.astype(BF16)
            lo = (r1 - mid.astype(F32)).astype(BF16)
            gc_all = _dot(tri, hi) + _dot(tri, mid) + _dot(tri, lo)
            gcs.append(gc_all)
            gcts.append(gc_all.T)
        gcol = [gcs[ch][:, A_LANE + hd:A_LANE + hd + 1] for ch, hd in sys]
        grow = [gcts[ch][A_LANE + hd:A_LANE + hd + 1, :] for ch, hd in sys]
        beta = [gbts[ch][:, BETA_LANE + hd:BETA_LANE + hd + 1] for ch, hd in sys]
        glast = [gcs[ch][c - 1:c, A_LANE + hd:A_LANE + hd + 1] for ch, hd in sys]
        decay = [jnp.where(incl, jnp.exp(jnp.where(incl, a - b, 0.0)), 0.0) for a, b in zip(gcol, grow)]
        kb = [k_ref[0, pl.ds(r0s[ch], c), lanes[hd]] for ch, hd in sys]
        qb = [q_ref[0, pl.ds(r0s[ch], c), lanes[hd]] for ch, hd in sys]
        kq = [_dot_nt(jnp.concatenate([k, q], axis=0), k) for k, q in zip(kb, qb)]
        a_mat = [x[:c] * b * d for x, b, d in zip(kq, beta, decay)]
        attn = [(x[c:] * d).astype(BF16) for x, d in zip(kq, decay)]
        dmat = [jnp.where(diag_mask, a, 0.0) for a in a_mat]
        t = [eye - d for d in dmat]
        p = [(-d).astype(BF16) for d in dmat]
        p = [_dot(x, x) for x in p]
        for _ in range(base_shift - 2):
            pb = [x.astype(BF16) for x in p]
            tp = [_dot(jnp.concatenate([x.astype(BF16), y], axis=0), y) for x, y in zip(t, pb)]
            t = [x + y[:c] for x, y in zip(t, tp)]
            p = [y[c:] for y in tp]
        t = [x + _dot(x.astype(BF16), y.astype(BF16)) for x, y in zip(t, p)]
        for mask in merge_masks:
            tb = [x.astype(BF16) for x in t]
            lt = [_dot(jnp.where(mask, a, 0.0).astype(BF16), x) for a, x in zip(a_mat, tb)]
            t = [x - _dot(xb, y.astype(BF16)) for x, xb, y in zip(t, tb, lt)]
        egc = [jnp.exp(x) for x in gcol]
        kf = [x.astype(F32) for x in kb]
        vf = [v_ref[0, pl.ds(r0s[ch], c), lanes[hd]].astype(F32) for ch, hd in sys]
        rhs = [jnp.concatenate([(k * (b * e)).astype(BF16), (v * b).astype(BF16)], axis=1)
               for k, v, b, e in zip(kf, vf, beta, egc)]
        wu = [_dot(x.astype(BF16), r) for x, r in zip(t, rhs)]
        qg = [(q.astype(F32) * e).astype(BF16) for q, e in zip(qb, egc)]
        kdt = [(k * jnp.exp(gl - gc)).T.astype(BF16) for k, gl, gc in zip(kf, glast, gcol)]
        for i, (ch, hd) in enumerate(sys):
            sc = slot * group + ch
            rows = slice(sc * c, (sc + 1) * c)
            w_ref[rows, lanes[hd]] = wu[i][:, :DN_DK].astype(BF16)
            u_ref[rows, lanes[hd]] = wu[i][:, DN_DK:]
            qg_ref[rows, lanes[hd]] = qg[i]
            attn_ref[hd, rows, :] = attn[i]
            kdt_ref[hd, sc * DN_DK:(sc + 1) * DN_DK, :] = kdt[i]
            egl_ref[sc * DN_HEADS + hd:sc * DN_HEADS + hd + 1, :] = jnp.broadcast_to(jnp.exp(glast[i]), (1, LANES))

    def recur(gi, slot):
        for ch in range(group):
            sc = slot * group + ch
            rows = slice(sc * c, (sc + 1) * c)
            io_rows = pl.ds(_aligned((gi * group + ch) * c, c), c)
            state = [state_ref[hd] for hd in heads]
            ws = [_dot(jnp.concatenate([w_ref[rows, lanes[hd]], qg_ref[rows, lanes[hd]]], axis=0),
                       state[hd].astype(BF16)) for hd in heads]
            v_new = [(u_ref[rows, lanes[hd]] - ws[hd][:c]).astype(BF16) for hd in heads]
            o = [ws[hd][c:] + _dot(attn_ref[hd, rows, :], v_new[hd]) for hd in heads]
            for hd in heads:
                egl = egl_ref[sc * DN_HEADS + hd:sc * DN_HEADS + hd + 1, :]
                state_ref[hd] = state[hd] * egl + _dot(kdt_ref[hd, sc * DN_DK:(sc + 1) * DN_DK, :], v_new[hd])
            for hd in heads:
                on = o[hd] * lax.rsqrt(jnp.mean(o[hd] * o[hd], axis=-1, keepdims=True) + 1e-6) * gnorm
                o_ref[0, io_rows, lanes[hd]] = (on * zs_ref[0, io_rows, lanes[hd]].astype(F32)).astype(BF16)

    n_groups = n_chunks // group
    prepare(0, 0)

    def pair(t, carry):
        prepare(2 * t + 1, 1)
        recur(2 * t, 0)
        prepare(2 * t + 2, 0)
        recur(2 * t + 1, 1)
        return carry

    lax.fori_loop(0, n_groups // 2 - 1, pair, 0)
    prepare(n_groups - 1, 1)
    recur(n_groups - 2, 0)
    recur(n_groups - 1, 1)


def _gdn(q_dn, k_dn, v_dn, gb, zs, gnorm, *, tg, group):
    bsz, s, _ = q_dn.shape
    n_chunks = tg // GDN_CHUNK
    assert n_chunks % (2 * group) == 0
    slots = 2 * group
    row = lambda w: pl.BlockSpec((1, tg, w), lambda b, i: (b, i, 0))
    return pl.pallas_call(
        functools.partial(_gdn_kernel, n_chunks=n_chunks, group=group),
        grid=(bsz, s // tg),
        in_specs=[row(DN_QK), row(DN_QK), row(DN_VW), row(LANES), row(DN_VW), _const_spec(gnorm.shape)],
        out_specs=row(DN_VW),
        out_shape=jax.ShapeDtypeStruct((bsz, s, DN_VW), BF16),
        scratch_shapes=[pltpu.VMEM((DN_HEADS, DN_DK, DN_DV), F32),
                        pltpu.VMEM((slots * GDN_CHUNK, DN_QK), BF16),
                        pltpu.VMEM((slots * GDN_CHUNK, DN_VW), F32),
                        pltpu.VMEM((slots * GDN_CHUNK, DN_QK), BF16),
                        pltpu.VMEM((DN_HEADS, slots * GDN_CHUNK, GDN_CHUNK), BF16),
                        pltpu.VMEM((DN_HEADS, slots * DN_DK, GDN_CHUNK), BF16),
                        pltpu.VMEM((slots * DN_HEADS, LANES), F32)],
        compiler_params=pltpu.CompilerParams(dimension_semantics=("arbitrary", "arbitrary"),
                                             vmem_limit_bytes=VMEM_LIMIT),
        name="gdn",
    )(q_dn, k_dn, v_dn, gb, zs, gnorm)


NEG_BIG = -1e30


def _attn_kernel(q_ref, k_ref, vt_ref, o_ref, s_ref, m_ref, l_ref, acc_ref, *, tq):
    qi = pl.program_id(2)
    q = q_ref[0]
    m_ref[...] = jnp.full(m_ref.shape, NEG_BIG, F32)
    l_ref[...] = jnp.zeros(l_ref.shape, F32)
    acc_ref[...] = jnp.zeros(acc_ref.shape, F32)

    def scores(j):
        return _dot_nt(k_ref[0, pl.ds(pl.multiple_of(j * tq, tq), tq), :], q)

    def update(j, s):
        m_old = m_ref[...]
        m_new = jnp.maximum(m_old, jnp.max(s, axis=0, keepdims=True))
        alpha = jnp.exp2(m_old - m_new)
        p = jnp.exp2(s - m_new)
        l_ref[...] = l_ref[...] * alpha + jnp.sum(p, axis=0, keepdims=True)
        acc_ref[...] = acc_ref[...] * alpha + _dot(vt_ref[0, 0, j], p.astype(BF16))
        m_ref[...] = m_new

    s_ref[0] = scores(0)

    def step(j, slot):
        s_ref[1 - slot] = scores(j + 1)
        update(j, s_ref[slot])

    def body(t, carry):
        step(2 * t, 0)
        step(2 * t + 1, 1)
        return carry

    lax.fori_loop(0, qi // 2, body, 0)

    @pl.when(qi % 2 == 1)
    def _():
        step(qi - 1, 0)

    s = s_ref[qi & 1]
    kidx = lax.broadcasted_iota(jnp.int32, s.shape, 0)
    qidx = lax.broadcasted_iota(jnp.int32, s.shape, 1)
    update(qi, jnp.where(kidx <= qidx, s, NEG_BIG))
    o_ref[0] = (acc_ref[...] / l_ref[...]).T.astype(BF16)


def _attention(q_att, k_att, vt, *, tq):
    bsz, s, _ = q_att.shape
    nblk = s // tq
    return pl.pallas_call(
        functools.partial(_attn_kernel, tq=tq),
        grid=(bsz, MLA_HEADS, nblk),
        in_specs=[pl.BlockSpec((1, tq, QATT_W), lambda b, h, i: (b, i, h)),
                  pl.BlockSpec((1, s, QATT_W), lambda b, h, i: (b, 0, h)),
                  pl.BlockSpec((1, 1, nblk, V_HEAD, tq), lambda b, h, i: (b, h, 0, 0, 0))],
        out_specs=pl.BlockSpec((1, tq, V_HEAD), lambda b, h, i: (b, i, h)),
        out_shape=jax.ShapeDtypeStruct((bsz, s, MLA_HEADS * V_HEAD), BF16),
        scratch_shapes=[pltpu.VMEM((2, tq, tq), F32),
                        pltpu.VMEM((1, tq), F32), pltpu.VMEM((1, tq), F32), pltpu.VMEM((V_HEAD, tq), F32)],
        compiler_params=pltpu.CompilerParams(dimension_semantics=("arbitrary", "arbitrary", "arbitrary"),
                                             vmem_limit_bytes=VMEM_LIMIT),
        name="mla_attention",
    )(q_att, k_att, vt)


def _layernorm(y, g, b):
    mu = jnp.mean(y, axis=-1, keepdims=True)
    yc = y - mu
    var = jnp.mean(yc * yc, axis=-1, keepdims=True)
    return yc * lax.rsqrt(var + 1e-5) * g + b


def _out_kernel(x_ref, og_ref, om_ref, mod_ref, wo_ref, ln_ref, wg_ref, wu_ref, wd_ref, o_ref, act_ref,
                *, alpha, ff_chunk):
    x = x_ref[0]
    gt_m = mod_ref[0, 2:3, :]
    sh_f = mod_ref[0, 3:4, :]
    sc_f = mod_ref[0, 4:5, :]
    gt_f = mod_ref[0, 5:6, :]
    nw = og_ref.shape[-1]
    mix = _dot(og_ref[0], wo_ref[0:nw, :]) + _dot(om_ref[0], wo_ref[nw:, :])
    x1 = _layernorm(alpha * x + gt_m * mix, ln_ref[0:1, :], ln_ref[1:2, :])
    h = (x1 * (1.0 + sc_f) + sh_f).astype(BF16)
    d_ff = wg_ref.shape[1]
    for lo in range(0, d_ff, ff_chunk):
        g = _dot(h, wg_ref[:, lo:lo + ff_chunk])
        u = _dot(h, wu_ref[:, lo:lo + ff_chunk])
        act_ref[:, lo:lo + ff_chunk] = (_silu(g) * u).astype(BF16)
    ff = _dot(act_ref[...], wd_ref[...])
    o_ref[0] = _layernorm(alpha * x1 + gt_f * ff, ln_ref[2:3, :], ln_ref[3:4, :])


def _out_ffn(x, og, om, mod, wo, ln, wg, wu, wd, *, tm, alpha, ff_chunk):
    bsz, s, d = x.shape
    d_ff = wg.shape[1]
    row = lambda w: pl.BlockSpec((1, tm, w), lambda b, i: (b, i, 0))
    return pl.pallas_call(
        functools.partial(_out_kernel, alpha=alpha, ff_chunk=ff_chunk),
        grid=(bsz, s // tm),
        in_specs=[row(d), row(og.shape[-1]), row(om.shape[-1]),
                  pl.BlockSpec((1, mod.shape[1], d), lambda b, i: (b, 0, 0)),
                  _const_spec(wo.shape), _const_spec(ln.shape),
                  _const_spec(wg.shape), _const_spec(wu.shape), _const_spec(wd.shape)],
        out_specs=row(d),
        out_shape=jax.ShapeDtypeStruct((bsz, s, d), F32),
        scratch_shapes=[pltpu.VMEM((tm, d_ff), BF16)],
        compiler_params=pltpu.CompilerParams(dimension_semantics=("arbitrary", "arbitrary"),
                                             vmem_limit_bytes=VMEM_LIMIT),
        name="out_ffn",
    )(x, og, om, mod, wo, ln, wg, wu, wd)


def _pack_w_in(w_in):
    d = w_in.shape[0]
    split_z = DN_CONV_CH
    split_beta = split_z + DN_VW
    split_a = split_beta + DN_HEADS
    split_cq = split_a + DN_HEADS
    split_ckv = split_cq + Q_LORA
    split_kr = split_ckv + KV_LORA
    kr = w_in[:, split_kr:split_kr + QK_ROPE]
    zeros = lambda n: jnp.zeros((d, n), w_in.dtype)
    misc = jnp.concatenate([
        kr[:, :HALF_ROPE], zeros(KR2_LANE - HALF_ROPE),
        kr[:, HALF_ROPE:], zeros(BETA_LANE - KR2_LANE - HALF_ROPE),
        w_in[:, split_beta:split_a], w_in[:, split_a:split_cq],
        zeros(LANES - A_LANE - DN_HEADS)], axis=1)
    return jnp.concatenate([w_in[:, :split_beta], w_in[:, split_cq:split_kr], misc], axis=1).astype(BF16)


def _pack_w_uq(w_uq):
    q_lora = w_uq.shape[0]
    w = w_uq.reshape(q_lora, MLA_HEADS, QK_NOPE + QK_ROPE)
    zeros = lambda n: jnp.zeros((q_lora, MLA_HEADS, n), w_uq.dtype)
    tile = jnp.concatenate([
        w[:, :, :QK_NOPE],
        w[:, :, QK_NOPE:QK_NOPE + HALF_ROPE], zeros(KR2_LANE - HALF_ROPE),
        w[:, :, QK_NOPE + HALF_ROPE:], zeros(LANES - KR2_LANE - HALF_ROPE)], axis=2)
    return tile.reshape(q_lora, MLA_HEADS * QATT_W).astype(BF16)


def _pack_w_ukv(w_ukv):
    kv_lora = w_ukv.shape[0]
    w = w_ukv.reshape(kv_lora, MLA_HEADS, QK_NOPE + V_HEAD)
    return jnp.concatenate([w[:, :, :QK_NOPE].reshape(kv_lora, -1),
                            w[:, :, QK_NOPE:].reshape(kv_lora, -1)], axis=1).astype(BF16)


def _lane_table(a_log, dt_bias):
    z = jnp.zeros((LANES,), F32)
    ones = jnp.ones((HALF_ROPE,), F32)
    put = lambda vec, lane: z.at[lane:lane + vec.shape[0]].set(vec)
    rows = [put(-ones, KR1_LANE) + put(ones, KR2_LANE),
            put(dt_bias.astype(F32), A_LANE),
            put(a_log.astype(F32), A_LANE)]
    rows += [z] * (SUBLANES - len(rows))
    return jnp.stack(rows)


def _rotary_kernel(pos_ref, freq_ref, cos_ref, sin_ref):
    ang = pos_ref[...] * freq_ref[...]
    cos_ref[...] = jnp.cos(ang)
    sin_ref[...] = jnp.sin(ang)


def _rotary_tiles(positions):
    bsz, s = positions.shape
    per_row = LANES // HALF_ROPE
    rows = bsz * s // per_row
    inv_freq = 1.0 / (ROPE_THETA ** (jnp.arange(0, QK_ROPE, 2, dtype=F32) / QK_ROPE))
    pos_dense = jnp.repeat(positions.astype(F32).reshape(rows, per_row), HALF_ROPE, axis=1)
    blk = min(rows, 1024)
    cos, sin = pl.pallas_call(
        _rotary_kernel,
        grid=(rows // blk,),
        in_specs=[pl.BlockSpec((blk, LANES), lambda i: (i, 0)), pl.BlockSpec((1, LANES), lambda i: (0, 0))],
        out_specs=(pl.BlockSpec((blk, LANES), lambda i: (i, 0)), pl.BlockSpec((blk, LANES), lambda i: (i, 0))),
        out_shape=(jax.ShapeDtypeStruct((rows, LANES), F32), jax.ShapeDtypeStruct((rows, LANES), F32)),
        compiler_params=pltpu.CompilerParams(dimension_semantics=("arbitrary",)),
        name="rotary_tables",
    )(pos_dense, jnp.tile(inv_freq, per_row).reshape(1, LANES))
    cos = cos.reshape(bsz, s, HALF_ROPE)
    sin = sin.reshape(bsz, s, HALF_ROPE)
    gap1 = jnp.zeros((bsz, s, KR2_LANE - HALF_ROPE), F32)
    gap2 = jnp.zeros((bsz, s, LANES - KR2_LANE - HALF_ROPE), F32)
    return jnp.concatenate([cos, gap1, cos, gap2, sin, gap1, sin, gap2], axis=-1)


def _layer(x, c_mod, rot, w_in, conv_w, a_log, dt_bias, dn_norm_g, q_norm_g, w_uq, kv_norm_g, w_ukv,
           w_o, ln1_g, ln1_b, w_gate, w_up, w_down, ln2_g, ln2_b, *, depth, tm, tg, gdn_group, tq, ff_chunk):
    alpha = (2.0 * depth) ** 0.25
    q_dn, k_dn, v_dn, zs, gb, q_att, k_att, vt = _inproj(
        x, c_mod, rot, _pack_w_in(w_in), conv_w.reshape(CONV_K, DN_CONV_CH).astype(F32),
        _lane_table(a_log, dt_bias), q_norm_g.reshape(1, -1).astype(F32), kv_norm_g.reshape(1, -1).astype(F32),
        _pack_w_uq(w_uq), _pack_w_ukv(w_ukv), tm=tq)
    og = _gdn(q_dn, k_dn, v_dn, gb, zs, dn_norm_g.reshape(1, -1).astype(F32), tg=tg, group=gdn_group)
    om = _attention(q_att, k_att, vt, tq=tq)
    ln = jnp.stack([ln1_g, ln1_b, ln2_g, ln2_b]).astype(F32)
    return _out_ffn(x, og, om, c_mod, w_o.astype(BF16), ln, w_gate.astype(BF16), w_up.astype(BF16),
                    w_down.astype(BF16), tm=tm, alpha=alpha, ff_chunk=ff_chunk)


def kernel(x, c, positions, w_ada, b_ada, w_in, conv_w, a_log, dt_bias, dn_norm_g, q_norm_g, w_uq, kv_norm_g, w_ukv, w_o, ln1_g, ln1_b, w_gate, w_up, w_down, ln2_g, ln2_b):
    bsz, s, d = x.shape
    depth = w_in.shape[0]
    tile = min(512, s)
    rot = _rotary_tiles(positions)
    for l in range(depth):
        mod = _modulation(c, w_ada[l], b_ada[l]).reshape(bsz, 6, d)
        x = _layer(x, mod, rot, w_in[l], conv_w[l], a_log[l], dt_bias[l], dn_norm_g[l], q_norm_g[l], w_uq[l],
                   kv_norm_g[l], w_ukv[l], w_o[l], ln1_g[l], ln1_b[l], w_gate[l], w_up[l], w_down[l],
                   ln2_g[l], ln2_b[l], depth=depth, tm=tile, tg=min(GDN_TILE, s), gdn_group=GDN_GROUP, tq=tile, ff_chunk=256)
    return x
```

```python
import functools
import math

import jax
import jax.numpy as jnp
from jax import lax
from jax.experimental import pallas as pl
from jax.experimental.pallas import tpu as pltpu

F32 = jnp.float32
BF16 = jnp.bfloat16

DN_HEADS = 4
DN_DK = 128
DN_DV = 128
CONV_K = 4
MLA_HEADS = 4
QK_NOPE = 128
QK_ROPE = 64
V_HEAD = 128
Q_LORA = 512
KV_LORA = 256
ROPE_THETA = 10000.0

DN_QK = DN_HEADS * DN_DK
DN_VW = DN_HEADS * DN_DV
DN_CONV_CH = 2 * DN_QK + DN_VW

LANES = 128
SUBLANES = 8
MXU_COLS = 256
GDN_CHUNK = 64
GDN_BASE = 8
GDN_TILE = 1024
GDN_GROUP = 4
VMEM_LIMIT = 56 * 1024 * 1024

HALF_ROPE = QK_ROPE // 2
KR1_LANE = 0
KR2_LANE = 64
BETA_LANE = 96
A_LANE = 100
COL_QKV = 0
COL_Z = DN_CONV_CH
COL_CQ = COL_Z + DN_VW
COL_CKV = COL_CQ + Q_LORA
COL_MISC = COL_CKV + KV_LORA
N_IN_PAD = COL_MISC + LANES
QATT_W = 2 * LANES


def _dot(a, b):
    return jnp.dot(a, b, preferred_element_type=F32)


def _dot_f32(a, b):
    return jnp.dot(a, b, preferred_element_type=F32, precision=lax.Precision.HIGHEST)


def _dot_nt(a, b):
    return lax.dot_general(a, b, (((1,), (1,)), ((), ())), preferred_element_type=F32)


def _sigmoid(x):
    return 0.5 + 0.5 * jnp.tanh(0.5 * x)


def _silu(x):
    half = 0.5 * x
    return half + half * jnp.tanh(half)


def _aligned(start, multiple):
    return start if isinstance(start, int) else pl.multiple_of(start, multiple)


def _const_spec(shape):
    zeros = (0,) * len(shape)
    return pl.BlockSpec(shape, lambda *_: zeros, pipeline_mode=pl.Buffered(1))


def _mod_kernel(c_ref, w_ref, b_ref, o_ref):
    o_ref[...] = _dot(_silu(c_ref[...]), w_ref[...]) + b_ref[...]


def _modulation(c, w_ada, b_ada):
    bsz, d = c.shape
    n = w_ada.shape[1]
    return pl.pallas_call(
        _mod_kernel,
        grid=(n // d,),
        in_specs=[pl.BlockSpec((bsz, d), lambda j: (0, 0)),
                  pl.BlockSpec((d, d), lambda j: (0, j)),
                  pl.BlockSpec((1, d), lambda j: (0, j))],
        out_specs=pl.BlockSpec((bsz, d), lambda j: (0, j)),
        out_shape=jax.ShapeDtypeStruct((bsz, n), F32),
        compiler_params=pltpu.CompilerParams(dimension_semantics=("arbitrary",),
                                             vmem_limit_bytes=VMEM_LIMIT),
        name="modulation",
    )(c, w_ada, b_ada.reshape(1, n))


def _inproj_kernel(x_ref, mod_ref, pos_ref, freq_ref, win_ref, convw_ref, lane_ref, qng_ref, kvng_ref,
                   wuq_ref, wukv_ref,
                   qdn_ref, kdn_ref, vdn_ref, zs_ref, gb_ref, qatt_ref, katt_ref, vt_ref,
                   halo_ref, *, tm):
    i = pl.program_id(1)
    x = x_ref[0]
    sh_m = mod_ref[0, 0:1, :]
    sc_m = mod_ref[0, 1:2, :]
    h = (x * (1.0 + sc_m) + sh_m).astype(BF16)

    @pl.when(i == 0)
    def _():
        halo_ref[0:SUBLANES, :] = jnp.zeros((SUBLANES, DN_CONV_CH), F32)

    for lo in range(0, DN_CONV_CH, MXU_COLS):
        cols = slice(lo, lo + MXU_COLS)
        halo_ref[SUBLANES:SUBLANES + tm, cols] = _dot(h, win_ref[:, COL_QKV + lo:COL_QKV + lo + MXU_COLS])
        conv = halo_ref[SUBLANES:SUBLANES + tm, cols] * convw_ref[CONV_K - 1:CONV_K, cols]
        for d in range(1, CONV_K):
            conv = conv + halo_ref[SUBLANES - d:SUBLANES - d + tm, cols] * convw_ref[CONV_K - 1 - d:CONV_K - d, cols]
        act = _silu(conv)
        if lo >= 2 * DN_QK:
            vdn_ref[0, :, lo - 2 * DN_QK:lo - 2 * DN_QK + MXU_COLS] = act.astype(BF16)
            continue
        dst, off, gain = (qdn_ref, lo, DN_DK ** -0.5) if lo < DN_QK else (kdn_ref, lo - DN_QK, 1.0)
        for sub in range(0, MXU_COLS, DN_DK):
            t = act[:, sub:sub + DN_DK]
            t = t * (lax.rsqrt(jnp.sum(t * t, axis=-1, keepdims=True) + 1e-6) * gain)
            dst[0, :, off + sub:off + sub + DN_DK] = t.astype(BF16)
    halo_ref[0:SUBLANES, :] = halo_ref[tm:tm + SUBLANES, :]

    for lo in range(0, DN_VW, MXU_COLS):
        zs_ref[0, :, lo:lo + MXU_COLS] = _silu(_dot(h, win_ref[:, COL_Z + lo:COL_Z + lo + MXU_COLS])).astype(BF16)

    dt_bias = lane_ref[0:1, :]
    neg_a = -jnp.exp(lane_ref[1:2, :])
    ang = freq_ref[...] * pos_ref[0]
    cos_f = jnp.cos(ang)
    sin_f = jnp.sin(ang)
    gap1 = jnp.zeros((KR2_LANE - HALF_ROPE, tm), F32)
    gap2 = jnp.zeros((LANES - KR2_LANE - HALF_ROPE, tm), F32)
    cos_t = jnp.concatenate([cos_f, gap1, cos_f, gap2], axis=0).T
    sin_t = jnp.concatenate([-sin_f, gap1, sin_f, gap2], axis=0).T

    def rope(t):
        return t * cos_t + pltpu.roll(t, KR2_LANE - KR1_LANE, 1) * sin_t

    misc = _dot(h, win_ref[:, COL_MISC:COL_MISC + LANES])
    k_rope = rope(misc).astype(BF16)
    lane = lax.broadcasted_iota(jnp.int32, misc.shape, 1)
    a_in = misc + dt_bias
    softplus = jnp.maximum(a_in, 0.0) + jnp.log(1.0 + jnp.exp(-jnp.abs(a_in)))
    gb = jnp.where((lane >= BETA_LANE) & (lane < BETA_LANE + DN_HEADS), _sigmoid(misc),
                   jnp.where((lane >= A_LANE) & (lane < A_LANE + DN_HEADS), neg_a * softplus, 0.0))
    gb_ref[0] = gb

    cq = _dot(h, win_ref[:, COL_CQ:COL_CQ + Q_LORA])
    cq = cq * lax.rsqrt(jnp.mean(cq * cq, axis=-1, keepdims=True) + 1e-6) * qng_ref[...]
    qm = _dot(cq.astype(BF16), wuq_ref[...])
    scale = math.log2(math.e) / math.sqrt(QK_NOPE + QK_ROPE)
    for hd in range(MLA_HEADS):
        lo = hd * QATT_W
        qatt_ref[0, :, lo:lo + LANES] = (qm[:, lo:lo + LANES] * scale).astype(BF16)
        qatt_ref[0, :, lo + LANES:lo + QATT_W] = (rope(qm[:, lo + LANES:lo + QATT_W]) * scale).astype(BF16)

    ckv = _dot(h, win_ref[:, COL_CKV:COL_CKV + KV_LORA])
    ckv = ckv * lax.rsqrt(jnp.mean(ckv * ckv, axis=-1, keepdims=True) + 1e-6) * kvng_ref[...]
    kv = _dot(ckv.astype(BF16), wukv_ref[...])
    for hd in range(MLA_HEADS):
        lo = hd * QATT_W
        katt_ref[0, :, lo:lo + LANES] = kv[:, hd * QK_NOPE:(hd + 1) * QK_NOPE].astype(BF16)
        katt_ref[0, :, lo + LANES:lo + QATT_W] = k_rope
        v = kv[:, MLA_HEADS * QK_NOPE + hd * V_HEAD:MLA_HEADS * QK_NOPE + (hd + 1) * V_HEAD]
        vt_ref[0, hd, 0] = v.T.astype(BF16)


def _inproj(x, mod, pos, freq, win_p, convw, lane_tab, qng, kvng, wuq_p, wukv_p, *, tm):
    bsz, s, d = x.shape
    nblk = s // tm
    row = lambda w: pl.BlockSpec((1, tm, w), lambda b, i: (b, i, 0))
    out_shape = (
        jax.ShapeDtypeStruct((bsz, s, DN_QK), BF16),
        jax.ShapeDtypeStruct((bsz, s, DN_QK), BF16),
        jax.ShapeDtypeStruct((bsz, s, DN_VW), BF16),
        jax.ShapeDtypeStruct((bsz, s, DN_VW), BF16),
        jax.ShapeDtypeStruct((bsz, s, LANES), F32),
        jax.ShapeDtypeStruct((bsz, s, MLA_HEADS * QATT_W), BF16),
        jax.ShapeDtypeStruct((bsz, s, MLA_HEADS * QATT_W), BF16),
        jax.ShapeDtypeStruct((bsz, MLA_HEADS, nblk, V_HEAD, tm), BF16),
    )
    out_specs = (row(DN_QK), row(DN_QK), row(DN_VW), row(DN_VW), row(LANES),
                 row(MLA_HEADS * QATT_W), row(MLA_HEADS * QATT_W),
                 pl.BlockSpec((1, MLA_HEADS, 1, V_HEAD, tm), lambda b, i: (b, 0, i, 0, 0)))
    return pl.pallas_call(
        functools.partial(_inproj_kernel, tm=tm),
        grid=(bsz, nblk),
        in_specs=[row(d),
                  pl.BlockSpec((1, mod.shape[1], d), lambda b, i: (b, 0, 0)),
                  pl.BlockSpec((1, 1, tm), lambda b, i: (b, 0, i)),
                  _const_spec(freq.shape), _const_spec(win_p.shape), _const_spec(convw.shape), _const_spec(lane_tab.shape),
                  _const_spec(qng.shape), _const_spec(kvng.shape),
                  _const_spec(wuq_p.shape), _const_spec(wukv_p.shape)],
        out_specs=out_specs,
        out_shape=out_shape,
        scratch_shapes=[pltpu.VMEM((tm + 2 * SUBLANES, DN_CONV_CH), F32)],
        compiler_params=pltpu.CompilerParams(dimension_semantics=("arbitrary", "arbitrary"),
                                             vmem_limit_bytes=VMEM_LIMIT),
        name="inproj",
    )(x, mod, pos, freq, win_p, convw, lane_tab, qng, kvng, wuq_p, wukv_p)


def _gdn_kernel(q_ref, k_ref, v_ref, gb_ref, zs_ref, gn_ref, o_ref,
                state_ref, w_ref, u_ref, qg_ref, attn_ref, kdt_ref, egl_ref, *, n_chunks, group):
    c = GDN_CHUNK

    @pl.when(pl.program_id(1) == 0)
    def _():
        state_ref[...] = jnp.zeros(state_ref.shape, F32)

    row = lax.broadcasted_iota(jnp.int32, (c, c), 0)
    col = lax.broadcasted_iota(jnp.int32, (c, c), 1)
    incl = row >= col
    strict = row > col
    tri = incl.astype(BF16)
    eye = (row == col).astype(F32)
    gnorm = gn_ref[...]
    base_shift = int(math.log2(GDN_BASE))
    diag_mask = strict & ((row >> base_shift) == (col >> base_shift))
    merge_masks = []
    for sh in range(base_shift, int(math.log2(c))):
        merge_masks.append(((row >> sh) == (col >> sh) + 1) & (((row >> sh) & 1) == 1))
    heads = range(DN_HEADS)
    lanes = [slice(hd * DN_DK, (hd + 1) * DN_DK) for hd in heads]

    def prepare(gi, slot):
        sys = [(ch, hd) for ch in range(group) for hd in heads]
        r0s = [_aligned((gi * group + ch) * c, c) for ch in range(group)]
        gbts = [gb_ref[0, pl.ds(r0, c), :] for r0 in r0s]
        gcs, gcts = [], []
        for gbt in gbts:
            hi = gbt.astype(BF16)
            r1 = gbt - hi.astype(F32)
            mid = r1.astype(BF16)
            lo = (r1 - mid.astype(F32)).astype(BF16)
            gc_all = _dot(tri, hi) + _dot(tri, mid) + _dot(tri, lo)
            gcs.append(gc_all)
            gcts.append(gc_all.T)
        gcol = [gcs[ch][:, A_LANE + hd:A_LANE + hd + 1] for ch, hd in sys]
        grow = [gcts[ch][A_LANE + hd:A_LANE + hd + 1, :] for ch, hd in sys]
        beta = [gbts[ch][:, BETA_LANE + hd:BETA_LANE + hd + 1] for ch, hd in sys]
        glast = [gcs[ch][c - 1:c, A_LANE + hd:A_LANE + hd + 1] for ch, hd in sys]
        decay = [jnp.where(incl, jnp.exp(jnp.where(incl, a - b, 0.0)), 0.0) for a, b in zip(gcol, grow)]
        kb = [k_ref[0, pl.ds(r0s[ch], c), lanes[hd]] for ch, hd in sys]
        qb = [q_ref[0, pl.ds(r0s[ch], c), lanes[hd]] for ch, hd in sys]
        kq = [_dot_nt(jnp.concatenate([k, q], axis=0), k) for k, q in zip(kb, qb)]
        a_mat = [x[:c] * b * d for x, b, d in zip(kq, beta, decay)]
        attn = [(x[c:] * d).astype(BF16) for x, d in zip(kq, decay)]
        dmat = [jnp.where(diag_mask, a, 0.0) for a in a_mat]
        t = [eye - d for d in dmat]
        p = [(-d).astype(BF16) for d in dmat]
        p = [_dot(x, x) for x in p]
        for _ in range(base_shift - 2):
            pb = [x.astype(BF16) for x in p]
            tp = [_dot(jnp.concatenate([x.astype(BF16), y], axis=0), y) for x, y in zip(t, pb)]
            t = [x + y[:c] for x, y in zip(t, tp)]
            p = [y[c:] for y in tp]
        t = [x + _dot(x.astype(BF16), y.astype(BF16)) for x, y in zip(t, p)]
        for mask in merge_masks:
            tb = [x.astype(BF16) for x in t]
            lt = [_dot(jnp.where(mask, a, 0.0).astype(BF16), x) for a, x in zip(a_mat, tb)]
            t = [x - _dot(xb, y.astype(BF16)) for x, xb, y in zip(t, tb, lt)]
        egc = [jnp.exp(x) for x in gcol]
        kf = [x.astype(F32) for x in kb]
        vf = [v_ref[0, pl.ds(r0s[ch], c), lanes[hd]].astype(F32) for ch, hd in sys]
        rhs = [jnp.concatenate([(k * (b * e)).astype(BF16), (v * b).astype(BF16)], axis=1)
               for k, v, b, e in zip(kf, vf, beta, egc)]
        wu = [_dot(x.astype(BF16), r) for x, r in zip(t, rhs)]
        qg = [(q.astype(F32) * e).astype(BF16) for q, e in zip(qb, egc)]
        kdt = [(k * jnp.exp(gl - gc)).T.astype(BF16) for k, gl, gc in zip(kf, glast, gcol)]
        for i, (ch, hd) in enumerate(sys):
            sc = slot * group + ch
            rows = slice(sc * c, (sc + 1) * c)
            w_ref[rows, lanes[hd]] = wu[i][:, :DN_DK].astype(BF16)
            u_ref[rows, lanes[hd]] = wu[i][:, DN_DK:]
            qg_ref[rows, lanes[hd]] = qg[i]
            attn_ref[hd, rows, :] = attn[i]
            kdt_ref[hd, sc * DN_DK:(sc + 1) * DN_DK, :] = kdt[i]
            egl_ref[sc * DN_HEADS + hd:sc * DN_HEADS + hd + 1, :] = jnp.broadcast_to(jnp.exp(glast[i]), (1, LANES))

    def recur(gi, slot):
        for ch in range(group):
            sc = slot * group + ch
            rows = slice(sc * c, (sc + 1) * c)
            io_rows = pl.ds(_aligned((gi * group + ch) * c, c), c)
            state = [state_ref[hd] for hd in heads]
            ws = [_dot(jnp.concatenate([w_ref[rows, lanes[hd]], qg_ref[rows, lanes[hd]]], axis=0),
                       state[hd].astype(BF16)) for hd in heads]
            v_new = [(u_ref[rows, lanes[hd]] - ws[hd][:c]).astype(BF16) for hd in heads]
            o = [ws[hd][c:] + _dot(attn_ref[hd, rows, :], v_new[hd]) for hd in heads]
            for hd in heads:
                egl = egl_ref[sc * DN_HEADS + hd:sc * DN_HEADS + hd + 1, :]
                state_ref[hd] = state[hd] * egl + _dot(kdt_ref[hd, sc * DN_DK:(sc + 1) * DN_DK, :], v_new[hd])
            for hd in heads:
                on = o[hd] * lax.rsqrt(jnp.mean(o[hd] * o[hd], axis=-1, keepdims=True) + 1e-6) * gnorm
                o_ref[0, io_rows, lanes[hd]] = (on * zs_ref[0, io_rows, lanes[hd]].astype(F32)).astype(BF16)

    n_groups = n_chunks // group
    prepare(0, 0)

    def pair(t, carry):
        prepare(2 * t + 1, 1)
        recur(2 * t, 0)
        prepare(2 * t + 2, 0)
        recur(2 * t + 1, 1)
        return carry

    lax.fori_loop(0, n_groups // 2 - 1, pair, 0)
    prepare(n_groups - 1, 1)
    recur(n_groups - 2, 0)
    recur(n_groups - 1, 1)


def _gdn(q_dn, k_dn, v_dn, gb, zs, gnorm, *, tg, group):
    bsz, s, _ = q_dn.shape
    n_chunks = tg // GDN_CHUNK
    assert n_chunks % (2 * group) == 0
    slots = 2 * group
    row = lambda w: pl.BlockSpec((1, tg, w), lambda b, i: (b, i, 0))
    return pl.pallas_call(
        functools.partial(_gdn_kernel, n_chunks=n_chunks, group=group),
        grid=(bsz, s // tg),
        in_specs=[row(DN_QK), row(DN_QK), row(DN_VW), row(LANES), row(DN_VW), _const_spec(gnorm.shape)],
        out_specs=row(DN_VW),
        out_shape=jax.ShapeDtypeStruct((bsz, s, DN_VW), BF16),
        scratch_shapes=[pltpu.VMEM((DN_HEADS, DN_DK, DN_DV), F32),
                        pltpu.VMEM((slots * GDN_CHUNK, DN_QK), BF16),
                        pltpu.VMEM((slots * GDN_CHUNK, DN_VW), F32),
                        pltpu.VMEM((slots * GDN_CHUNK, DN_QK), BF16),
                        pltpu.VMEM((DN_HEADS, slots * GDN_CHUNK, GDN_CHUNK), BF16),
                        pltpu.VMEM((DN_HEADS, slots * DN_DK, GDN_CHUNK), BF16),
                        pltpu.VMEM((slots * DN_HEADS, LANES), F32)],
        compiler_params=pltpu.CompilerParams(dimension_semantics=("arbitrary", "arbitrary"),
                                             vmem_limit_bytes=VMEM_LIMIT),
        name="gdn",
    )(q_dn, k_dn, v_dn, gb, zs, gnorm)


NEG_BIG = -1e30
ATTN_HEADS_PER_STEP = 2


def _attn_kernel(q_ref, k_ref, vt_ref, o_ref, s_ref, m_ref, l_ref, acc_ref, bias_ref, *, tq, nh):
    qi = pl.program_id(2)
    heads = range(nh)

    @pl.when((pl.program_id(0) == 0) & (pl.program_id(1) == 0) & (qi == 0))
    def _():
        kidx = lax.broadcasted_iota(jnp.int32, (tq, tq), 0)
        qidx = lax.broadcasted_iota(jnp.int32, (tq, tq), 1)
        bias_ref[...] = jnp.where(kidx <= qidx, 0.0, NEG_BIG)

    q = [q_ref[0, :, h * QATT_W:(h + 1) * QATT_W] for h in heads]
    m_ref[...] = jnp.full(m_ref.shape, NEG_BIG, F32)
    l_ref[...] = jnp.zeros(l_ref.shape, F32)
    acc_ref[...] = jnp.zeros(acc_ref.shape, F32)

    def scores(j, h):
        rows = pl.ds(pl.multiple_of(j * tq, tq), tq)
        return _dot_nt(k_ref[0, rows, h * QATT_W:(h + 1) * QATT_W], q[h])

    def update(j, s):
        m_old = [m_ref[h] for h in heads]
        m_new = [jnp.maximum(m_old[h], jnp.max(s[h], axis=0, keepdims=True)) for h in heads]
        alpha = [jnp.exp2(m_old[h] - m_new[h]) for h in heads]
        p = [jnp.exp2(s[h] - m_new[h]) for h in heads]
        for h in heads:
            l_ref[h] = l_ref[h] * alpha[h] + jnp.sum(p[h], axis=0, keepdims=True)
            acc_ref[h] = acc_ref[h] * alpha[h] + _dot(vt_ref[0, h, j], p[h].astype(BF16))
            m_ref[h] = m_new[h]

    for h in heads:
        s_ref[h, 0] = scores(0, h)

    def step(j, slot):
        for h in heads:
            s_ref[h, 1 - slot] = scores(j + 1, h)
        update(j, [s_ref[h, slot] for h in heads])

    def body(t, carry):
        step(2 * t, 0)
        step(2 * t + 1, 1)
        return carry

    lax.fori_loop(0, qi // 2, body, 0)

    @pl.when(qi % 2 == 1)
    def _():
        step(qi - 1, 0)

    bias = bias_ref[...]
    update(qi, [s_ref[h, qi & 1] + bias for h in heads])
    for h in heads:
        o_ref[0, :, h * V_HEAD:(h + 1) * V_HEAD] = (acc_ref[h] / l_ref[h]).T.astype(BF16)


def _attention(q_att, k_att, vt, *, tq, nh):
    bsz, s, _ = q_att.shape
    nblk = s // tq
    return pl.pallas_call(
        functools.partial(_attn_kernel, tq=tq, nh=nh),
        grid=(bsz, MLA_HEADS // nh, nblk),
        in_specs=[pl.BlockSpec((1, tq, nh * QATT_W), lambda b, h, i: (b, i, h)),
                  pl.BlockSpec((1, s, nh * QATT_W), lambda b, h, i: (b, 0, h)),
                  pl.BlockSpec((1, nh, nblk, V_HEAD, tq), lambda b, h, i: (b, h, 0, 0, 0))],
        out_specs=pl.BlockSpec((1, tq, nh * V_HEAD), lambda b, h, i: (b, i, h)),
        out_shape=jax.ShapeDtypeStruct((bsz, s, MLA_HEADS * V_HEAD), BF16),
        scratch_shapes=[pltpu.VMEM((nh, 2, tq, tq), F32),
                        pltpu.VMEM((nh, 1, tq), F32), pltpu.VMEM((nh, 1, tq), F32),
                        pltpu.VMEM((nh, V_HEAD, tq), F32),
                        pltpu.VMEM((tq, tq), F32)],
        compiler_params=pltpu.CompilerParams(dimension_semantics=("arbitrary", "arbitrary", "arbitrary"),
                                             vmem_limit_bytes=VMEM_LIMIT),
        name="mla_attention",
    )(q_att, k_att, vt)


def _layernorm(y, g, b):
    mu = jnp.mean(y, axis=-1, keepdims=True)
    yc = y - mu
    var = jnp.mean(yc * yc, axis=-1, keepdims=True)
    return yc * lax.rsqrt(var + 1e-5) * g + b


def _out_kernel(x_ref, og_ref, om_ref, mod_ref, wo_ref, ln_ref, wg_ref, wu_ref, wd_ref, o_ref, act_ref,
                *, alpha, ff_chunk):
    gt_m = mod_ref[0, 2:3, :]
    sh_f = mod_ref[0, 3:4, :]
    sc_f = mod_ref[0, 4:5, :]
    gt_f = mod_ref[0, 5:6, :]
    nw = og_ref.shape[-1]
    x = x_ref[0]
    mix = _dot(og_ref[0], wo_ref[0:nw, :]) + _dot(om_ref[0], wo_ref[nw:, :])
    x1 = _layernorm(alpha * x + gt_m * mix, ln_ref[0:1, :], ln_ref[1:2, :])
    h = (x1 * (1.0 + sc_f) + sh_f).astype(BF16)
    d_ff = wg_ref.shape[1]
    for lo in range(0, d_ff, ff_chunk):
        g = _dot(h, wg_ref[:, lo:lo + ff_chunk])
        u = _dot(h, wu_ref[:, lo:lo + ff_chunk])
        act_ref[:, lo:lo + ff_chunk] = (_silu(g) * u).astype(BF16)
    ff = _dot(act_ref[...], wd_ref[...])
    o_ref[0] = _layernorm(alpha * x1 + gt_f * ff, ln_ref[2:3, :], ln_ref[3:4, :])


def _out_ffn(x, og, om, mod, wo, ln, wg, wu, wd, *, tm, alpha, ff_chunk):
    bsz, s, d = x.shape
    d_ff = wg.shape[1]
    row = lambda w: pl.BlockSpec((1, tm, w), lambda b, i: (b, i, 0))
    return pl.pallas_call(
        functools.partial(_out_kernel, alpha=alpha, ff_chunk=ff_chunk),
        grid=(bsz, s // tm),
        in_specs=[row(d), row(og.shape[-1]), row(om.shape[-1]),
                  pl.BlockSpec((1, mod.shape[1], d), lambda b, i: (b, 0, 0)),
                  _const_spec(wo.shape), _const_spec(ln.shape),
                  _const_spec(wg.shape), _const_spec(wu.shape), _const_spec(wd.shape)],
        out_specs=row(d),
        out_shape=jax.ShapeDtypeStruct((bsz, s, d), F32),
        scratch_shapes=[pltpu.VMEM((tm, d_ff), BF16)],
        compiler_params=pltpu.CompilerParams(dimension_semantics=("arbitrary", "arbitrary"),
                                             vmem_limit_bytes=VMEM_LIMIT),
        name="out_ffn",
    )(x, og, om, mod, wo, ln, wg, wu, wd)


def _pack_w_in(w_in):
    d = w_in.shape[0]
    split_z = DN_CONV_CH
    split_beta = split_z + DN_VW
    split_a = split_beta + DN_HEADS
    split_cq = split_a + DN_HEADS
    split_ckv = split_cq + Q_LORA
    split_kr = split_ckv + KV_LORA
    kr = w_in[:, split_kr:split_kr + QK_ROPE]
    zeros = lambda n: jnp.zeros((d, n), w_in.dtype)
    misc = jnp.concatenate([
        kr[:, :HALF_ROPE], zeros(KR2_LANE - HALF_ROPE),
        kr[:, HALF_ROPE:], zeros(BETA_LANE - KR2_LANE - HALF_ROPE),
        w_in[:, split_beta:split_a], w_in[:, split_a:split_cq],
        zeros(LANES - A_LANE - DN_HEADS)], axis=1)
    return jnp.concatenate([w_in[:, :split_beta], w_in[:, split_cq:split_kr], misc], axis=1).astype(BF16)


def _pack_w_uq(w_uq):
    q_lora = w_uq.shape[0]
    w = w_uq.reshape(q_lora, MLA_HEADS, QK_NOPE + QK_ROPE)
    zeros = lambda n: jnp.zeros((q_lora, MLA_HEADS, n), w_uq.dtype)
    tile = jnp.concatenate([
        w[:, :, :QK_NOPE],
        w[:, :, QK_NOPE:QK_NOPE + HALF_ROPE], zeros(KR2_LANE - HALF_ROPE),
        w[:, :, QK_NOPE + HALF_ROPE:], zeros(LANES - KR2_LANE - HALF_ROPE)], axis=2)
    return tile.reshape(q_lora, MLA_HEADS * QATT_W).astype(BF16)


def _pack_w_ukv(w_ukv):
    kv_lora = w_ukv.shape[0]
    w = w_ukv.reshape(kv_lora, MLA_HEADS, QK_NOPE + V_HEAD)
    return jnp.concatenate([w[:, :, :QK_NOPE].reshape(kv_lora, -1),
                            w[:, :, QK_NOPE:].reshape(kv_lora, -1)], axis=1).astype(BF16)


def _lane_table(a_log, dt_bias):
    z = jnp.zeros((LANES,), F32)
    put = lambda vec, lane: z.at[lane:lane + vec.shape[0]].set(vec)
    rows = [put(dt_bias.astype(F32), A_LANE), put(a_log.astype(F32), A_LANE)]
    rows += [z] * (SUBLANES - len(rows))
    return jnp.stack(rows)


def _inv_freq_column():
    inv_freq = 1.0 / (ROPE_THETA ** (jnp.arange(0, QK_ROPE, 2, dtype=F32) / QK_ROPE))
    return inv_freq.reshape(HALF_ROPE, 1)


def _layer(x, c_mod, pos, w_in, conv_w, a_log, dt_bias, dn_norm_g, q_norm_g, w_uq, kv_norm_g, w_ukv,
           w_o, ln1_g, ln1_b, w_gate, w_up, w_down, ln2_g, ln2_b, *, depth, tm, tg, gdn_group, tq, ff_chunk):
    alpha = (2.0 * depth) ** 0.25
    q_dn, k_dn, v_dn, zs, gb, q_att, k_att, vt = _inproj(
        x, c_mod, pos, _inv_freq_column(), _pack_w_in(w_in), conv_w.reshape(CONV_K, DN_CONV_CH).astype(F32),
        _lane_table(a_log, dt_bias), q_norm_g.reshape(1, -1).astype(F32), kv_norm_g.reshape(1, -1).astype(F32),
        _pack_w_uq(w_uq), _pack_w_ukv(w_ukv), tm=tq)
    og = _gdn(q_dn, k_dn, v_dn, gb, zs, dn_norm_g.reshape(1, -1).astype(F32), tg=tg, group=gdn_group)
    om = _attention(q_att, k_att, vt, tq=tq, nh=ATTN_HEADS_PER_STEP)
    ln = jnp.stack([ln1_g, ln1_b, ln2_g, ln2_b]).astype(F32)
    return _out_ffn(x, og, om, c_mod, w_o.astype(BF16), ln, w_gate.astype(BF16), w_up.astype(BF16),
                    w_down.astype(BF16), tm=tm, alpha=alpha, ff_chunk=ff_chunk)


def kernel(x, c, positions, w_ada, b_ada, w_in, conv_w, a_log, dt_bias, dn_norm_g, q_norm_g, w_uq, kv_norm_g, w_ukv, w_o, ln1_g, ln1_b, w_gate, w_up, w_down, ln2_g, ln2_b):
    bsz, s, d = x.shape
    depth = w_in.shape[0]
    tile = min(512, s)
    pos = positions.astype(F32).reshape(bsz, 1, s)
    for l in range(depth):
        mod = _modulation(c, w_ada[l], b_ada[l]).reshape(bsz, 6, d)
        x = _layer(x, mod, pos, w_in[l], conv_w[l], a_log[l], dt_bias[l], dn_norm_g[l], q_norm_g[l], w_uq[l],
                   kv_norm_g[l], w_ukv[l], w_o[l], ln1_g[l], ln1_b[l], w_gate[l], w_up[l], w_down[l],
                   ln2_g[l], ln2_b[l], depth=depth, tm=tile, tg=min(GDN_TILE, s), gdn_group=GDN_GROUP, tq=tile, ff_chunk=256)
    return x
```

```python
import functools
import math

import jax
import jax.numpy as jnp
from jax import lax
from jax.experimental import pallas as pl
from jax.experimental.pallas import tpu as pltpu

F32 = jnp.float32
BF16 = jnp.bfloat16

DN_HEADS = 4
DN_DK = 128
DN_DV = 128
CONV_K = 4
MLA_HEADS = 4
QK_NOPE = 128
QK_ROPE = 64
V_HEAD = 128
Q_LORA = 512
KV_LORA = 256
ROPE_THETA = 10000.0

DN_QK = DN_HEADS * DN_DK
DN_VW = DN_HEADS * DN_DV
DN_CONV_CH = 2 * DN_QK + DN_VW

LANES = 128
SUBLANES = 8
MXU_COLS = 256
GDN_CHUNK = 64
GDN_BASE = 8
GDN_TILE = 1024
GDN_GROUP = 4
VMEM_LIMIT = 56 * 1024 * 1024

HALF_ROPE = QK_ROPE // 2
KR1_LANE = 0
KR2_LANE = 64
BETA_LANE = 96
A_LANE = 100
COL_QKV = 0
COL_Z = DN_CONV_CH
COL_CQ = COL_Z + DN_VW
COL_CKV = COL_CQ + Q_LORA
COL_MISC = COL_CKV + KV_LORA
N_IN_PAD = COL_MISC + LANES
QATT_W = 2 * LANES


def _dot(a, b):
    return jnp.dot(a, b, preferred_element_type=F32)


def _dot_f32(a, b):
    return jnp.dot(a, b, preferred_element_type=F32, precision=lax.Precision.HIGHEST)


def _dot_nt(a, b):
    return lax.dot_general(a, b, (((1,), (1,)), ((), ())), preferred_element_type=F32)


def _sigmoid(x):
    return 0.5 + 0.5 * jnp.tanh(0.5 * x)


def _silu(x):
    half = 0.5 * x
    return half + half * jnp.tanh(half)


def _aligned(start, multiple):
    return start if isinstance(start, int) else pl.multiple_of(start, multiple)


def _const_spec(shape):
    zeros = (0,) * len(shape)
    return pl.BlockSpec(shape, lambda *_: zeros, pipeline_mode=pl.Buffered(1))


def _mod_kernel(c_ref, w_ref, b_ref, o_ref):
    o_ref[...] = _dot(_silu(c_ref[...]), w_ref[...]) + b_ref[...]


def _modulation(c, w_ada, b_ada):
    bsz, d = c.shape
    n = w_ada.shape[1]
    return pl.pallas_call(
        _mod_kernel,
        grid=(n // d,),
        in_specs=[pl.BlockSpec((bsz, d), lambda j: (0, 0)),
                  pl.BlockSpec((d, d), lambda j: (0, j)),
                  pl.BlockSpec((1, d), lambda j: (0, j))],
        out_specs=pl.BlockSpec((bsz, d), lambda j: (0, j)),
        out_shape=jax.ShapeDtypeStruct((bsz, n), F32),
        compiler_params=pltpu.CompilerParams(dimension_semantics=("arbitrary",),
                                             vmem_limit_bytes=VMEM_LIMIT),
        name="modulation",
    )(c, w_ada, b_ada.reshape(1, n))


def _inproj_kernel(x_ref, mod_ref, pos_ref, freq_ref, win_ref, convw_ref, lane_ref, qng_ref, kvng_ref,
                   wuq_ref, wukv_ref,
                   qdn_ref, kdn_ref, vdn_ref, zs_ref, gb_ref, qatt_ref, katt_ref, vt_ref,
                   halo_ref, *, tm):
    i = pl.program_id(1)
    sh_m = mod_ref[0, 0:1, :]
    sc_m = mod_ref[0, 1:2, :]
    dt_bias = lane_ref[0:1, :]
    neg_a = -jnp.exp(lane_ref[1:2, :])
    scale = math.log2(math.e) / math.sqrt(QK_NOPE + QK_ROPE)

    @pl.when(i == 0)
    def _():
        halo_ref[0:SUBLANES, :] = jnp.zeros((SUBLANES, DN_CONV_CH), F32)

    def rows_pass(r0, nr):
        rows = slice(r0, r0 + nr)
        hrows = slice(SUBLANES + r0, SUBLANES + r0 + nr)
        h = (x_ref[0, rows, :] * (1.0 + sc_m) + sh_m).astype(BF16)

        for lo in range(0, DN_CONV_CH, MXU_COLS):
            cols = slice(lo, lo + MXU_COLS)
            halo_ref[hrows, cols] = _dot(h, win_ref[:, COL_QKV + lo:COL_QKV + lo + MXU_COLS])
            conv = halo_ref[hrows, cols] * convw_ref[CONV_K - 1:CONV_K, cols]
            for d in range(1, CONV_K):
                shifted = halo_ref[SUBLANES + r0 - d:SUBLANES + r0 - d + nr, cols]
                conv = conv + shifted * convw_ref[CONV_K - 1 - d:CONV_K - d, cols]
            act = _silu(conv)
            if lo >= 2 * DN_QK:
                vdn_ref[0, rows, lo - 2 * DN_QK:lo - 2 * DN_QK + MXU_COLS] = act.astype(BF16)
                continue
            dst, off, gain = (qdn_ref, lo, DN_DK ** -0.5) if lo < DN_QK else (kdn_ref, lo - DN_QK, 1.0)
            for sub in range(0, MXU_COLS, DN_DK):
                t = act[:, sub:sub + DN_DK]
                t = t * (lax.rsqrt(jnp.sum(t * t, axis=-1, keepdims=True) + 1e-6) * gain)
                dst[0, rows, off + sub:off + sub + DN_DK] = t.astype(BF16)

        for lo in range(0, DN_VW, MXU_COLS):
            zs_ref[0, rows, lo:lo + MXU_COLS] = _silu(
                _dot(h, win_ref[:, COL_Z + lo:COL_Z + lo + MXU_COLS])).astype(BF16)

        ang = freq_ref[...] * pos_ref[0, :, rows]
        cos_f = jnp.cos(ang)
        sin_f = jnp.sin(ang)
        gap1 = jnp.zeros((KR2_LANE - HALF_ROPE, nr), F32)
        gap2 = jnp.zeros((LANES - KR2_LANE - HALF_ROPE, nr), F32)
        cos_t = jnp.concatenate([cos_f, gap1, cos_f, gap2], axis=0).T
        sin_t = jnp.concatenate([-sin_f, gap1, sin_f, gap2], axis=0).T

        def rope(t):
            return t * cos_t + pltpu.roll(t, KR2_LANE - KR1_LANE, 1) * sin_t

        misc = _dot(h, win_ref[:, COL_MISC:COL_MISC + LANES])
        k_rope = rope(misc).astype(BF16)
        lane = lax.broadcasted_iota(jnp.int32, misc.shape, 1)
        a_in = misc + dt_bias
        softplus = jnp.maximum(a_in, 0.0) + jnp.log(1.0 + jnp.exp(-jnp.abs(a_in)))
        gb_ref[0, rows, :] = jnp.where(
            (lane >= BETA_LANE) & (lane < BETA_LANE + DN_HEADS), _sigmoid(misc),
            jnp.where((lane >= A_LANE) & (lane < A_LANE + DN_HEADS), neg_a * softplus, 0.0))

        cq = _dot(h, win_ref[:, COL_CQ:COL_CQ + Q_LORA])
        cq = cq * lax.rsqrt(jnp.mean(cq * cq, axis=-1, keepdims=True) + 1e-6) * qng_ref[...]
        qm = _dot(cq.astype(BF16), wuq_ref[...])
        for hd in range(MLA_HEADS):
            lo = hd * QATT_W
            qatt_ref[0, rows, lo:lo + LANES] = (qm[:, lo:lo + LANES] * scale).astype(BF16)
            qatt_ref[0, rows, lo + LANES:lo + QATT_W] = (rope(qm[:, lo + LANES:lo + QATT_W]) * scale).astype(BF16)

        ckv = _dot(h, win_ref[:, COL_CKV:COL_CKV + KV_LORA])
        ckv = ckv * lax.rsqrt(jnp.mean(ckv * ckv, axis=-1, keepdims=True) + 1e-6) * kvng_ref[...]
        kv = _dot(ckv.astype(BF16), wukv_ref[...])
        for hd in range(MLA_HEADS):
            lo = hd * QATT_W
            katt_ref[0, rows, lo:lo + LANES] = kv[:, hd * QK_NOPE:(hd + 1) * QK_NOPE].astype(BF16)
            katt_ref[0, rows, lo + LANES:lo + QATT_W] = k_rope
            v = kv[:, MLA_HEADS * QK_NOPE + hd * V_HEAD:MLA_HEADS * QK_NOPE + (hd + 1) * V_HEAD]
            vt_ref[0, hd, 0, :, rows] = v.T.astype(BF16)

    rows_pass(0, tm)
    halo_ref[0:SUBLANES, :] = halo_ref[tm:tm + SUBLANES, :]


def _inproj(x, mod, pos, freq, win_p, convw, lane_tab, qng, kvng, wuq_p, wukv_p, *, tm):
    bsz, s, d = x.shape
    nblk = s // tm
    row = lambda w: pl.BlockSpec((1, tm, w), lambda b, i: (b, i, 0))
    out_shape = (
        jax.ShapeDtypeStruct((bsz, s, DN_QK), BF16),
        jax.ShapeDtypeStruct((bsz, s, DN_QK), BF16),
        jax.ShapeDtypeStruct((bsz, s, DN_VW), BF16),
        jax.ShapeDtypeStruct((bsz, s, DN_VW), BF16),
        jax.ShapeDtypeStruct((bsz, s, LANES), F32),
        jax.ShapeDtypeStruct((bsz, s, MLA_HEADS * QATT_W), BF16),
        jax.ShapeDtypeStruct((bsz, s, MLA_HEADS * QATT_W), BF16),
        jax.ShapeDtypeStruct((bsz, MLA_HEADS, nblk, V_HEAD, tm), BF16),
    )
    out_specs = (row(DN_QK), row(DN_QK), row(DN_VW), row(DN_VW), row(LANES),
                 row(MLA_HEADS * QATT_W), row(MLA_HEADS * QATT_W),
                 pl.BlockSpec((1, MLA_HEADS, 1, V_HEAD, tm), lambda b, i: (b, 0, i, 0, 0)))
    return pl.pallas_call(
        functools.partial(_inproj_kernel, tm=tm),
        grid=(bsz, nblk),
        in_specs=[row(d),
                  pl.BlockSpec((1, mod.shape[1], d), lambda b, i: (b, 0, 0)),
                  pl.BlockSpec((1, 1, tm), lambda b, i: (b, 0, i)),
                  _const_spec(freq.shape), _const_spec(win_p.shape), _const_spec(convw.shape), _const_spec(lane_tab.shape),
                  _const_spec(qng.shape), _const_spec(kvng.shape),
                  _const_spec(wuq_p.shape), _const_spec(wukv_p.shape)],
        out_specs=out_specs,
        out_shape=out_shape,
        scratch_shapes=[pltpu.VMEM((tm + 2 * SUBLANES, DN_CONV_CH), F32)],
        compiler_params=pltpu.CompilerParams(dimension_semantics=("arbitrary", "arbitrary"),
                                             vmem_limit_bytes=VMEM_LIMIT),
        name="inproj",
    )(x, mod, pos, freq, win_p, convw, lane_tab, qng, kvng, wuq_p, wukv_p)


def _gdn_kernel(q_ref, k_ref, v_ref, gb_ref, zs_ref, gn_ref, o_ref,
                state_ref, w_ref, u_ref, qg_ref, attn_ref, kdt_ref, egl_ref, *, n_chunks, group):
    c = GDN_CHUNK

    @pl.when(pl.program_id(1) == 0)
    def _():
        state_ref[...] = jnp.zeros(state_ref.shape, F32)

    row = lax.broadcasted_iota(jnp.int32, (c, c), 0)
    col = lax.broadcasted_iota(jnp.int32, (c, c), 1)
    incl = row >= col
    strict = row > col
    tri = incl.astype(BF16)
    eye = (row == col).astype(F32)
    gnorm = gn_ref[...]
    base_shift = int(math.log2(GDN_BASE))
    diag_mask = strict & ((row >> base_shift) == (col >> base_shift))
    merge_masks = []
    for sh in range(base_shift, int(math.log2(c))):
        merge_masks.append(((row >> sh) == (col >> sh) + 1) & (((row >> sh) & 1) == 1))
    heads = range(DN_HEADS)
    lanes = [slice(hd * DN_DK, (hd + 1) * DN_DK) for hd in heads]

    def prepare(gi, slot):
        sys = [(ch, hd) for ch in range(group) for hd in heads]
        r0s = [_aligned((gi * group + ch) * c, c) for ch in range(group)]
        gbts = [gb_ref[0, pl.ds(r0, c), :] for r0 in r0s]
        gcs, gcts = [], []
        for gbt in gbts:
            hi = gbt.astype(BF16)
            r1 = gbt - hi.astype(F32)
            mid = r1.astype(BF16)
            lo = (r1 - mid.astype(F32)).astype(BF16)
            gc_all = _dot(tri, hi) + _dot(tri, mid) + _dot(tri, lo)
            gcs.append(gc_all)
            gcts.append(gc_all.T)
        gcol = [gcs[ch][:, A_LANE + hd:A_LANE + hd + 1] for ch, hd in sys]
        grow = [gcts[ch][A_LANE + hd:A_LANE + hd + 1, :] for ch, hd in sys]
        beta = [gbts[ch][:, BETA_LANE + hd:BETA_LANE + hd + 1] for ch, hd in sys]
        glast = [gcs[ch][c - 1:c, A_LANE + hd:A_LANE + hd + 1] for ch, hd in sys]
        decay = [jnp.where(incl, jnp.exp(jnp.where(incl, a - b, 0.0)), 0.0) for a, b in zip(gcol, grow)]
        kb = [k_ref[0, pl.ds(r0s[ch], c), lanes[hd]] for ch, hd in sys]
        qb = [q_ref[0, pl.ds(r0s[ch], c), lanes[hd]] for ch, hd in sys]
        kq = [_dot_nt(jnp.concatenate([k, q], axis=0), k) for k, q in zip(kb, qb)]
        a_mat = [x[:c] * b * d for x, b, d in zip(kq, beta, decay)]
        attn = [(x[c:] * d).astype(BF16) for x, d in zip(kq, decay)]
        dmat = [jnp.where(diag_mask, a, 0.0) for a in a_mat]
        t = [eye - d for d in dmat]
        p = [(-d).astype(BF16) for d in dmat]
        p = [_dot(x, x) for x in p]
        for _ in range(base_shift - 2):
            pb = [x.astype(BF16) for x in p]
            tp = [_dot(jnp.concatenate([x.astype(BF16), y], axis=0), y) for x, y in zip(t, pb)]
            t = [x + y[:c] for x, y in zip(t, tp)]
            p = [y[c:] for y in tp]
        t = [x + _dot(x.astype(BF16), y.astype(BF16)) for x, y in zip(t, p)]
        for mask in merge_masks:
            tb = [x.astype(BF16) for x in t]
            lt = [_dot(jnp.where(mask, a, 0.0).astype(BF16), x) for a, x in zip(a_mat, tb)]
            t = [x - _dot(xb, y.astype(BF16)) for x, xb, y in zip(t, tb, lt)]
        egc = [jnp.exp(x) for x in gcol]
        kf = [x.astype(F32) for x in kb]
        vf = [v_ref[0, pl.ds(r0s[ch], c), lanes[hd]].astype(F32) for ch, hd in sys]
        rhs = [jnp.concatenate([(k * (b * e)).astype(BF16), (v * b).astype(BF16)], axis=1)
               for k, v, b, e in zip(kf, vf, beta, egc)]
        wu = [_dot(x.astype(BF16), r) for x, r in zip(t, rhs)]
        qg = [(q.astype(F32) * e).astype(BF16) for q, e in zip(qb, egc)]
        kdt = [(k * jnp.exp(gl - gc)).T.astype(BF16) for k, gl, gc in zip(kf, glast, gcol)]
        for i, (ch, hd) in enumerate(sys):
            sc = slot * group + ch
            rows = slice(sc * c, (sc + 1) * c)
            w_ref[rows, lanes[hd]] = wu[i][:, :DN_DK].astype(BF16)
            u_ref[rows, lanes[hd]] = wu[i][:, DN_DK:]
            qg_ref[rows, lanes[hd]] = qg[i]
            attn_ref[hd, rows, :] = attn[i]
            kdt_ref[hd, sc * DN_DK:(sc + 1) * DN_DK, :] = kdt[i]
            egl_ref[sc * DN_HEADS + hd:sc * DN_HEADS + hd + 1, :] = jnp.broadcast_to(jnp.exp(glast[i]), (1, LANES))

    def recur(gi, slot):
        for ch in range(group):
            sc = slot * group + ch
            rows = slice(sc * c, (sc + 1) * c)
            io_rows = pl.ds(_aligned((gi * group + ch) * c, c), c)
            state = [state_ref[hd] for hd in heads]
            ws = [_dot(jnp.concatenate([w_ref[rows, lanes[hd]], qg_ref[rows, lanes[hd]]], axis=0),
                       state[hd].astype(BF16)) for hd in heads]
            v_new = [(u_ref[rows, lanes[hd]] - ws[hd][:c]).astype(BF16) for hd in heads]
            o = [ws[hd][c:] + _dot(attn_ref[hd, rows, :], v_new[hd]) for hd in heads]
            for hd in heads:
                egl = egl_ref[sc * DN_HEADS + hd:sc * DN_HEADS + hd + 1, :]
                state_ref[hd] = state[hd] * egl + _dot(kdt_ref[hd, sc * DN_DK:(sc + 1) * DN_DK, :], v_new[hd])
            for hd in heads:
                on = o[hd] * lax.rsqrt(jnp.mean(o[hd] * o[hd], axis=-1, keepdims=True) + 1e-6) * gnorm
                o_ref[0, io_rows, lanes[hd]] = (on * zs_ref[0, io_rows, lanes[hd]].astype(F32)).astype(BF16)

    n_groups = n_chunks // group
    prepare(0, 0)

    def pair(t, carry):
        prepare(2 * t + 1, 1)
        recur(2 * t, 0)
        prepare(2 * t + 2, 0)
        recur(2 * t + 1, 1)
        return carry

    lax.fori_loop(0, n_groups // 2 - 1, pair, 0)
    prepare(n_groups - 1, 1)
    recur(n_groups - 2, 0)
    recur(n_groups - 1, 1)


def _gdn(q_dn, k_dn, v_dn, gb, zs, gnorm, *, tg, group):
    bsz, s, _ = q_dn.shape
    n_chunks = tg // GDN_CHUNK
    assert n_chunks % (2 * group) == 0
    slots = 2 * group
    row = lambda w: pl.BlockSpec((1, tg, w), lambda b, i: (b, i, 0))
    return pl.pallas_call(
        functools.partial(_gdn_kernel, n_chunks=n_chunks, group=group),
        grid=(bsz, s // tg),
        in_specs=[row(DN_QK), row(DN_QK), row(DN_VW), row(LANES), row(DN_VW), _const_spec(gnorm.shape)],
        out_specs=row(DN_VW),
        out_shape=jax.ShapeDtypeStruct((bsz, s, DN_VW), BF16),
        scratch_shapes=[pltpu.VMEM((DN_HEADS, DN_DK, DN_DV), F32),
                        pltpu.VMEM((slots * GDN_CHUNK, DN_QK), BF16),
                        pltpu.VMEM((slots * GDN_CHUNK, DN_VW), F32),
                        pltpu.VMEM((slots * GDN_CHUNK, DN_QK), BF16),
                        pltpu.VMEM((DN_HEADS, slots * GDN_CHUNK, GDN_CHUNK), BF16),
                        pltpu.VMEM((DN_HEADS, slots * DN_DK, GDN_CHUNK), BF16),
                        pltpu.VMEM((slots * DN_HEADS, LANES), F32)],
        compiler_params=pltpu.CompilerParams(dimension_semantics=("arbitrary", "arbitrary"),
                                             vmem_limit_bytes=VMEM_LIMIT),
        name="gdn",
    )(q_dn, k_dn, v_dn, gb, zs, gnorm)


NEG_BIG = -1e30
ATTN_HEADS_PER_STEP = 4
FFN_PIECES = 8
BF16_ROWS = 16
HUGE = 3.0e38


def _attn_kernel(q_ref, k_ref, vt_ref, o_ref, s_ref, m_ref, l_ref, acc_ref, bias_ref, *, tq, nh):
    qi = pl.program_id(2)
    heads = range(nh)

    @pl.when((pl.program_id(0) == 0) & (pl.program_id(1) == 0) & (qi == 0))
    def _():
        kidx = lax.broadcasted_iota(jnp.int32, (tq, tq), 0)
        qidx = lax.broadcasted_iota(jnp.int32, (tq, tq), 1)
        bias_ref[...] = jnp.where(kidx <= qidx, 0.0, NEG_BIG)

    q = [q_ref[0, :, h * QATT_W:(h + 1) * QATT_W] for h in heads]
    m_ref[...] = jnp.full(m_ref.shape, NEG_BIG, F32)
    l_ref[...] = jnp.zeros(l_ref.shape, F32)
    acc_ref[...] = jnp.zeros(acc_ref.shape, F32)

    def scores(j, h):
        rows = pl.ds(pl.multiple_of(j * tq, tq), tq)
        return _dot_nt(k_ref[0, rows, h * QATT_W:(h + 1) * QATT_W], q[h])

    def update(j, s):
        m_old = [m_ref[h] for h in heads]
        m_new = [jnp.maximum(m_old[h], jnp.max(s[h], axis=0, keepdims=True)) for h in heads]
        alpha = [jnp.exp2(m_old[h] - m_new[h]) for h in heads]
        p = [jnp.exp2(s[h] - m_new[h]) for h in heads]
        for h in heads:
            l_ref[h] = l_ref[h] * alpha[h] + jnp.sum(p[h], axis=0, keepdims=True)
            acc_ref[h] = acc_ref[h] * alpha[h] + _dot(vt_ref[0, h, j], p[h].astype(BF16))
            m_ref[h] = m_new[h]

    for h in heads:
        s_ref[h, 0] = scores(0, h)

    def step(j, slot):
        for h in heads:
            s_ref[h, 1 - slot] = scores(j + 1, h)
        update(j, [s_ref[h, slot] for h in heads])

    def body(t, carry):
        step(2 * t, 0)
        step(2 * t + 1, 1)
        return carry

    lax.fori_loop(0, qi // 2, body, 0)

    @pl.when(qi % 2 == 1)
    def _():
        step(qi - 1, 0)

    bias = bias_ref[...]
    update(qi, [s_ref[h, qi & 1] + bias for h in heads])
    for h in heads:
        o_ref[0, :, h * V_HEAD:(h + 1) * V_HEAD] = (acc_ref[h] / l_ref[h]).T.astype(BF16)


def _attention(q_att, k_att, vt, *, tq, nh):
    bsz, s, _ = q_att.shape
    nblk = s // tq
    return pl.pallas_call(
        functools.partial(_attn_kernel, tq=tq, nh=nh),
        grid=(bsz, MLA_HEADS // nh, nblk),
        in_specs=[pl.BlockSpec((1, tq, nh * QATT_W), lambda b, h, i: (b, i, h)),
                  pl.BlockSpec((1, s, nh * QATT_W), lambda b, h, i: (b, 0, h)),
                  pl.BlockSpec((1, nh, nblk, V_HEAD, tq), lambda b, h, i: (b, h, 0, 0, 0))],
        out_specs=pl.BlockSpec((1, tq, nh * V_HEAD), lambda b, h, i: (b, i, h)),
        out_shape=jax.ShapeDtypeStruct((bsz, s, MLA_HEADS * V_HEAD), BF16),
        scratch_shapes=[pltpu.VMEM((nh, 2, tq, tq), F32),
                        pltpu.VMEM((nh, 1, tq), F32), pltpu.VMEM((nh, 1, tq), F32),
                        pltpu.VMEM((nh, V_HEAD, tq), F32),
                        pltpu.VMEM((tq, tq), F32)],
        compiler_params=pltpu.CompilerParams(dimension_semantics=("arbitrary", "arbitrary", "arbitrary"),
                                             vmem_limit_bytes=VMEM_LIMIT),
        name="mla_attention",
    )(q_att, k_att, vt)


def _layernorm(y, g, b):
    mu = jnp.mean(y, axis=-1, keepdims=True)
    yc = y - mu
    var = jnp.mean(yc * yc, axis=-1, keepdims=True)
    return yc * lax.rsqrt(var + 1e-5) * g + b


def _out_kernel(x_ref, og_ref, om_ref, mod_in_ref, mod_out_ref, wo_ref, ln_ref, wg_ref, wu_ref, wd_ref, o_ref,
                act_ref, x1_ref, h_ref, ff_ref, *, alpha, ff_chunk, n_blocks):
    n = pl.program_id(0)
    nw = og_ref.shape[-1]
    d_ff = wg_ref.shape[1]

    tm = x_ref.shape[1]

    def head_rows(slot, rows, mix):
        gt_m = mod_in_ref[0, 2:3, :]
        sh_f = mod_in_ref[0, 3:4, :]
        sc_f = mod_in_ref[0, 4:5, :]
        x1 = _layernorm(alpha * x_ref[0, rows, :] + gt_m * mix, ln_ref[0:1, :], ln_ref[1:2, :])
        h = (x1 * (1.0 + sc_f) + sh_f).astype(BF16)
        x1_ref[slot, rows, :] = x1
        h_ref[slot, rows, :] = h
        return h

    def mixer_proj():
        return _dot(og_ref[0], wo_ref[0:nw, :]) + _dot(om_ref[0], wo_ref[nw:, :])

    def head(slot):
        head_rows(slot, slice(0, tm), mixer_proj())

    def tail_rows(slot, rows):
        gt_f = mod_out_ref[0, 5:6, :]
        o = _layernorm(alpha * x1_ref[slot, rows, :] + gt_f * ff_ref[slot, rows, :], ln_ref[2:3, :], ln_ref[3:4, :])
        o_ref[0, rows, :] = o
        return o

    def tail(slot):
        tail_rows(slot, slice(0, tm))

    def ffn(slot, gates=None):
        for k, lo in enumerate(range(0, d_ff, ff_chunk)):
            h = h_ref[slot]
            if gates is not None and gates[k] is not None:
                h = jnp.concatenate([h[:BF16_ROWS] + gates[k], h[BF16_ROWS:]], axis=0)
            g = _dot(h, wg_ref[:, lo:lo + ff_chunk])
            u = _dot(h, wu_ref[:, lo:lo + ff_chunk])
            act_ref[:, lo:lo + ff_chunk] = (_silu(g) * u).astype(BF16)
        ff_ref[slot] = _dot(act_ref[...], wd_ref[...])

    def zero_after(*values):
        flag = None
        for v in values:
            v = v.astype(F32)
            top = v[0:SUBLANES]
            for r in range(SUBLANES, v.shape[0], SUBLANES):
                top = jnp.maximum(top, v[r:r + SUBLANES])
            over = top > HUGE
            flag = over if flag is None else (flag | over)
        z = jnp.where(flag, 1.0, 0.0)
        return jnp.concatenate([z, z], axis=0).astype(BF16)

    @pl.when(n == 0)
    def _():
        x1_ref[1] = jnp.zeros(x1_ref.shape[1:], F32)
        ff_ref[1] = jnp.zeros(ff_ref.shape[1:], F32)
        head(0)

    def steady(p):
        n_chunks = -(-d_ff // ff_chunk)
        mix = mixer_proj()
        gates = [None] * n_chunks
        piece = tm // FFN_PIECES
        for k in range(FFN_PIECES):
            rows = slice(k * piece, (k + 1) * piece)
            o = tail_rows(p, rows)
            h = head_rows(p, rows, mix[k * piece:(k + 1) * piece])
            gates[k + n_chunks - FFN_PIECES - 1] = zero_after(o, h)
        ffn(1 - p, gates)

    @pl.when((n >= 1) & (n <= n_blocks) & (n % 2 == 0))
    def _():
        steady(0)

    @pl.when((n >= 1) & (n <= n_blocks) & (n % 2 == 1))
    def _():
        steady(1)

    @pl.when(n == n_blocks + 1)
    def _():
        tail((n_blocks + 1) % 2)


def _out_ffn(x, og, om, mod, wo, ln, wg, wu, wd, *, tm, alpha, ff_chunk):
    bsz, s, d = x.shape
    d_ff = wg.shape[1]
    nb = s // tm
    n_blocks = bsz * nb
    blk_in = lambda n: jnp.minimum(n, n_blocks - 1)
    blk_out = lambda n: jnp.clip(n - 2, 0, n_blocks - 1)
    row_in = lambda w: pl.BlockSpec((1, tm, w), lambda n: (blk_in(n) // nb, blk_in(n) % nb, 0))
    mod_spec = lambda blk: pl.BlockSpec((1, mod.shape[1], d), lambda n: (blk(n) // nb, 0, 0))
    return pl.pallas_call(
        functools.partial(_out_kernel, alpha=alpha, ff_chunk=ff_chunk, n_blocks=n_blocks),
        grid=(n_blocks + 2,),
        in_specs=[row_in(d), row_in(og.shape[-1]), row_in(om.shape[-1]),
                  mod_spec(blk_in), mod_spec(blk_out),
                  _const_spec(wo.shape), _const_spec(ln.shape),
                  _const_spec(wg.shape), _const_spec(wu.shape), _const_spec(wd.shape)],
        out_specs=pl.BlockSpec((1, tm, d), lambda n: (blk_out(n) // nb, blk_out(n) % nb, 0)),
        out_shape=jax.ShapeDtypeStruct((bsz, s, d), F32),
        scratch_shapes=[pltpu.VMEM((tm, d_ff), BF16),
                        pltpu.VMEM((2, tm, d), F32),
                        pltpu.VMEM((2, tm, d), BF16),
                        pltpu.VMEM((2, tm, d), F32)],
        compiler_params=pltpu.CompilerParams(dimension_semantics=("arbitrary",),
                                             vmem_limit_bytes=VMEM_LIMIT),
        name="out_ffn",
    )(x, og, om, mod, mod, wo, ln, wg, wu, wd)


def _pack_w_in(w_in):
    d = w_in.shape[0]
    split_z = DN_CONV_CH
    split_beta = split_z + DN_VW
    split_a = split_beta + DN_HEADS
    split_cq = split_a + DN_HEADS
    split_ckv = split_cq + Q_LORA
    split_kr = split_ckv + KV_LORA
    kr = w_in[:, split_kr:split_kr + QK_ROPE]
    zeros = lambda n: jnp.zeros((d, n), w_in.dtype)
    misc = jnp.concatenate([
        kr[:, :HALF_ROPE], zeros(KR2_LANE - HALF_ROPE),
        kr[:, HALF_ROPE:], zeros(BETA_LANE - KR2_LANE - HALF_ROPE),
        w_in[:, split_beta:split_a], w_in[:, split_a:split_cq],
        zeros(LANES - A_LANE - DN_HEADS)], axis=1)
    return jnp.concatenate([w_in[:, :split_beta], w_in[:, split_cq:split_kr], misc], axis=1).astype(BF16)


def _pack_w_uq(w_uq):
    q_lora = w_uq.shape[0]
    w = w_uq.reshape(q_lora, MLA_HEADS, QK_NOPE + QK_ROPE)
    zeros = lambda n: jnp.zeros((q_lora, MLA_HEADS, n), w_uq.dtype)
    tile = jnp.concatenate([
        w[:, :, :QK_NOPE],
        w[:, :, QK_NOPE:QK_NOPE + HALF_ROPE], zeros(KR2_LANE - HALF_ROPE),
        w[:, :, QK_NOPE + HALF_ROPE:], zeros(LANES - KR2_LANE - HALF_ROPE)], axis=2)
    return tile.reshape(q_lora, MLA_HEADS * QATT_W).astype(BF16)


def _pack_w_ukv(w_ukv):
    kv_lora = w_ukv.shape[0]
    w = w_ukv.reshape(kv_lora, MLA_HEADS, QK_NOPE + V_HEAD)
    return jnp.concatenate([w[:, :, :QK_NOPE].reshape(kv_lora, -1),
                            w[:, :, QK_NOPE:].reshape(kv_lora, -1)], axis=1).astype(BF16)


def _lane_table(a_log, dt_bias):
    z = jnp.zeros((LANES,), F32)
    put = lambda vec, lane: z.at[lane:lane + vec.shape[0]].set(vec)
    rows = [put(dt_bias.astype(F32), A_LANE), put(a_log.astype(F32), A_LANE)]
    rows += [z] * (SUBLANES - len(rows))
    return jnp.stack(rows)


def _inv_freq_column():
    inv_freq = 1.0 / (ROPE_THETA ** (jnp.arange(0, QK_ROPE, 2, dtype=F32) / QK_ROPE))
    return inv_freq.reshape(HALF_ROPE, 1)


def _layer(x, c_mod, pos, w_in, conv_w, a_log, dt_bias, dn_norm_g, q_norm_g, w_uq, kv_norm_g, w_ukv,
           w_o, ln1_g, ln1_b, w_gate, w_up, w_down, ln2_g, ln2_b, *, depth, tm, tg, gdn_group, tq, ff_chunk):
    alpha = (2.0 * depth) ** 0.25
    q_dn, k_dn, v_dn, zs, gb, q_att, k_att, vt = _inproj(
        x, c_mod, pos, _inv_freq_column(), _pack_w_in(w_in), conv_w.reshape(CONV_K, DN_CONV_CH).astype(F32),
        _lane_table(a_log, dt_bias), q_norm_g.reshape(1, -1).astype(F32), kv_norm_g.reshape(1, -1).astype(F32),
        _pack_w_uq(w_uq), _pack_w_ukv(w_ukv), tm=tq)
    og = _gdn(q_dn, k_dn, v_dn, gb, zs, dn_norm_g.reshape(1, -1).astype(F32), tg=tg, group=gdn_group)
    om = _attention(q_att, k_att, vt, tq=tq, nh=ATTN_HEADS_PER_STEP)
    ln = jnp.stack([ln1_g, ln1_b, ln2_g, ln2_b]).astype(F32)
    return _out_ffn(x, og, om, c_mod, w_o.astype(BF16), ln, w_gate.astype(BF16), w_up.astype(BF16),
                    w_down.astype(BF16), tm=tm, alpha=alpha, ff_chunk=ff_chunk)


def kernel(x, c, positions, w_ada, b_ada, w_in, conv_w, a_log, dt_bias, dn_norm_g, q_norm_g, w_uq, kv_norm_g, w_ukv, w_o, ln1_g, ln1_b, w_gate, w_up, w_down, ln2_g, ln2_b):
    bsz, s, d = x.shape
    depth = w_in.shape[0]
    tile = min(512, s)
    pos = positions.astype(F32).reshape(bsz, 1, s)
    for l in range(depth):
        mod = _modulation(c, w_ada[l], b_ada[l]).reshape(bsz, 6, d)
        x = _layer(x, mod, pos, w_in[l], conv_w[l], a_log[l], dt_bias[l], dn_norm_g[l], q_norm_g[l], w_uq[l],
                   kv_norm_g[l], w_ukv[l], w_o[l], ln1_g[l], ln1_b[l], w_gate[l], w_up[l], w_down[l],
                   ln2_g[l], ln2_b[l], depth=depth, tm=tile, tg=min(GDN_TILE, s), gdn_group=GDN_GROUP, tq=tile, ff_chunk=256)
    return x
```

```python
import functools
import math

import jax
import jax.numpy as jnp
from jax import lax
from jax.experimental import pallas as pl
from jax.experimental.pallas import tpu as pltpu

F32 = jnp.float32
BF16 = jnp.bfloat16

DN_HEADS = 4
DN_DK = 128
DN_DV = 128
CONV_K = 4
MLA_HEADS = 4
QK_NOPE = 128
QK_ROPE = 64
V_HEAD = 128
Q_LORA = 512
KV_LORA = 256
ROPE_THETA = 10000.0

DN_QK = DN_HEADS * DN_DK
DN_VW = DN_HEADS * DN_DV
DN_CONV_CH = 2 * DN_QK + DN_VW

LANES = 128
SUBLANES = 8
MXU_COLS = 256
GDN_CHUNK = 64
GDN_BASE = 8
LHS_ROWS = DN_DK + GDN_CHUNK
GDN_TILE = 1024
GDN_GROUP = 4
VMEM_LIMIT = 56 * 1024 * 1024

HALF_ROPE = QK_ROPE // 2
KR1_LANE = 0
KR2_LANE = 64
BETA_LANE = 96
A_LANE = 100
COL_QKV = 0
COL_Z = DN_CONV_CH
COL_CQ = COL_Z + DN_VW
COL_CKV = COL_CQ + Q_LORA
COL_MISC = COL_CKV + KV_LORA
N_IN_PAD = COL_MISC + LANES
QATT_W = 2 * LANES


def _dot(a, b):
    return jnp.dot(a, b, preferred_element_type=F32)


def _dot_f32(a, b):
    return jnp.dot(a, b, preferred_element_type=F32, precision=lax.Precision.HIGHEST)


def _dot_nt(a, b):
    return lax.dot_general(a, b, (((1,), (1,)), ((), ())), preferred_element_type=F32)


def _sigmoid(x):
    return 0.5 + 0.5 * jnp.tanh(0.5 * x)


def _silu(x):
    half = 0.5 * x
    return half + half * jnp.tanh(half)


def _aligned(start, multiple):
    return start if isinstance(start, int) else pl.multiple_of(start, multiple)


def _const_spec(shape):
    zeros = (0,) * len(shape)
    return pl.BlockSpec(shape, lambda *_: zeros, pipeline_mode=pl.Buffered(1))


def _mod_kernel(c_ref, w_ref, b_ref, o_ref):
    o_ref[...] = _dot(_silu(c_ref[...]), w_ref[...]) + b_ref[...]


def _modulation(c, w_ada, b_ada):
    bsz, d = c.shape
    n = w_ada.shape[1]
    return pl.pallas_call(
        _mod_kernel,
        grid=(n // d,),
        in_specs=[pl.BlockSpec((bsz, d), lambda j: (0, 0)),
                  pl.BlockSpec((d, d), lambda j: (0, j)),
                  pl.BlockSpec((1, d), lambda j: (0, j))],
        out_specs=pl.BlockSpec((bsz, d), lambda j: (0, j)),
        out_shape=jax.ShapeDtypeStruct((bsz, n), F32),
        compiler_params=pltpu.CompilerParams(dimension_semantics=("arbitrary",),
                                             vmem_limit_bytes=VMEM_LIMIT),
        name="modulation",
    )(c, w_ada, b_ada.reshape(1, n))


def _inproj_kernel(x_ref, mod_ref, pos_ref, freq_ref, win_ref, convw_ref, lane_ref, qng_ref, kvng_ref,
                   wuq_ref, wukv_ref,
                   qdn_ref, kdn_ref, vdn_ref, zs_ref, gb_ref, qatt_ref, katt_ref, vt_ref,
                   halo_ref, *, tm):
    i = pl.program_id(1)
    sh_m = mod_ref[0, 0:1, :]
    sc_m = mod_ref[0, 1:2, :]
    dt_bias = lane_ref[0:1, :]
    neg_a = -jnp.exp(lane_ref[1:2, :])
    scale = math.log2(math.e) / math.sqrt(QK_NOPE + QK_ROPE)

    @pl.when(i == 0)
    def _():
        halo_ref[0:SUBLANES, :] = jnp.zeros((SUBLANES, DN_CONV_CH), F32)

    def rows_pass(r0, nr):
        rows = slice(r0, r0 + nr)
        hrows = slice(SUBLANES + r0, SUBLANES + r0 + nr)
        h = (x_ref[0, rows, :] * (1.0 + sc_m) + sh_m).astype(BF16)

        for lo in range(0, DN_CONV_CH, MXU_COLS):
            cols = slice(lo, lo + MXU_COLS)
            halo_ref[hrows, cols] = _dot(h, win_ref[:, COL_QKV + lo:COL_QKV + lo + MXU_COLS])
            conv = halo_ref[hrows, cols] * convw_ref[CONV_K - 1:CONV_K, cols]
            for d in range(1, CONV_K):
                shifted = halo_ref[SUBLANES + r0 - d:SUBLANES + r0 - d + nr, cols]
                conv = conv + shifted * convw_ref[CONV_K - 1 - d:CONV_K - d, cols]
            act = _silu(conv)
            if lo >= 2 * DN_QK:
                vdn_ref[0, rows, lo - 2 * DN_QK:lo - 2 * DN_QK + MXU_COLS] = act.astype(BF16)
                continue
            dst, off, gain = (qdn_ref, lo, DN_DK ** -0.5) if lo < DN_QK else (kdn_ref, lo - DN_QK, 1.0)
            for sub in range(0, MXU_COLS, DN_DK):
                t = act[:, sub:sub + DN_DK]
                t = t * (lax.rsqrt(jnp.sum(t * t, axis=-1, keepdims=True) + 1e-6) * gain)
                dst[0, rows, off + sub:off + sub + DN_DK] = t.astype(BF16)

        for lo in range(0, DN_VW, MXU_COLS):
            zs_ref[0, rows, lo:lo + MXU_COLS] = _silu(
                _dot(h, win_ref[:, COL_Z + lo:COL_Z + lo + MXU_COLS])).astype(BF16)

        ang = freq_ref[...] * pos_ref[0, :, rows]
        cos_f = jnp.cos(ang)
        sin_f = jnp.sin(ang)
        gap1 = jnp.zeros((KR2_LANE - HALF_ROPE, nr), F32)
        gap2 = jnp.zeros((LANES - KR2_LANE - HALF_ROPE, nr), F32)
        cos_t = jnp.concatenate([cos_f, gap1, cos_f, gap2], axis=0).T
        sin_t = jnp.concatenate([-sin_f, gap1, sin_f, gap2], axis=0).T

        def rope(t):
            return t * cos_t + pltpu.roll(t, KR2_LANE - KR1_LANE, 1) * sin_t

        misc = _dot(h, win_ref[:, COL_MISC:COL_MISC + LANES])
        k_rope = rope(misc).astype(BF16)
        lane = lax.broadcasted_iota(jnp.int32, misc.shape, 1)
        a_in = misc + dt_bias
        softplus = jnp.maximum(a_in, 0.0) + jnp.log(1.0 + jnp.exp(-jnp.abs(a_in)))
        gb_ref[0, rows, :] = jnp.where(
            (lane >= BETA_LANE) & (lane < BETA_LANE + DN_HEADS), _sigmoid(misc),
            jnp.where((lane >= A_LANE) & (lane < A_LANE + DN_HEADS), neg_a * softplus, 0.0))

        cq = _dot(h, win_ref[:, COL_CQ:COL_CQ + Q_LORA])
        cq = cq * lax.rsqrt(jnp.mean(cq * cq, axis=-1, keepdims=True) + 1e-6) * qng_ref[...]
        qm = _dot(cq.astype(BF16), wuq_ref[...])
        for hd in range(MLA_HEADS):
            lo = hd * QATT_W
            qatt_ref[0, rows, lo:lo + LANES] = (qm[:, lo:lo + LANES] * scale).astype(BF16)
            qatt_ref[0, rows, lo + LANES:lo + QATT_W] = (rope(qm[:, lo + LANES:lo + QATT_W]) * scale).astype(BF16)

        ckv = _dot(h, win_ref[:, COL_CKV:COL_CKV + KV_LORA])
        ckv = ckv * lax.rsqrt(jnp.mean(ckv * ckv, axis=-1, keepdims=True) + 1e-6) * kvng_ref[...]
        kv = _dot(ckv.astype(BF16), wukv_ref[...])
        for hd in range(MLA_HEADS):
            lo = hd * QATT_W
            katt_ref[0, rows, lo:lo + LANES] = kv[:, hd * QK_NOPE:(hd + 1) * QK_NOPE].astype(BF16)
            katt_ref[0, rows, lo + LANES:lo + QATT_W] = k_rope
            v = kv[:, MLA_HEADS * QK_NOPE + hd * V_HEAD:MLA_HEADS * QK_NOPE + (hd + 1) * V_HEAD]
            vt_ref[0, hd, 0, :, rows] = v.T.astype(BF16)

    rows_pass(0, tm)
    halo_ref[0:SUBLANES, :] = halo_ref[tm:tm + SUBLANES, :]


def _inproj(x, mod, pos, freq, win_p, convw, lane_tab, qng, kvng, wuq_p, wukv_p, *, tm):
    bsz, s, d = x.shape
    nblk = s // tm
    row = lambda w: pl.BlockSpec((1, tm, w), lambda b, i: (b, i, 0))
    out_shape = (
        jax.ShapeDtypeStruct((bsz, s, DN_QK), BF16),
        jax.ShapeDtypeStruct((bsz, s, DN_QK), BF16),
        jax.ShapeDtypeStruct((bsz, s, DN_VW), BF16),
        jax.ShapeDtypeStruct((bsz, s, DN_VW), BF16),
        jax.ShapeDtypeStruct((bsz, s, LANES), F32),
        jax.ShapeDtypeStruct((bsz, s, MLA_HEADS * QATT_W), BF16),
        jax.ShapeDtypeStruct((bsz, s, MLA_HEADS * QATT_W), BF16),
        jax.ShapeDtypeStruct((bsz, MLA_HEADS, nblk, V_HEAD, tm), BF16),
    )
    out_specs = (row(DN_QK), row(DN_QK), row(DN_VW), row(DN_VW), row(LANES),
                 row(MLA_HEADS * QATT_W), row(MLA_HEADS * QATT_W),
                 pl.BlockSpec((1, MLA_HEADS, 1, V_HEAD, tm), lambda b, i: (b, 0, i, 0, 0)))
    return pl.pallas_call(
        functools.partial(_inproj_kernel, tm=tm),
        grid=(bsz, nblk),
        in_specs=[row(d),
                  pl.BlockSpec((1, mod.shape[1], d), lambda b, i: (b, 0, 0)),
                  pl.BlockSpec((1, 1, tm), lambda b, i: (b, 0, i)),
                  _const_spec(freq.shape), _const_spec(win_p.shape), _const_spec(convw.shape), _const_spec(lane_tab.shape),
                  _const_spec(qng.shape), _const_spec(kvng.shape),
                  _const_spec(wuq_p.shape), _const_spec(wukv_p.shape)],
        out_specs=out_specs,
        out_shape=out_shape,
        scratch_shapes=[pltpu.VMEM((tm + 2 * SUBLANES, DN_CONV_CH), F32)],
        compiler_params=pltpu.CompilerParams(dimension_semantics=("arbitrary", "arbitrary"),
                                             vmem_limit_bytes=VMEM_LIMIT),
        name="inproj",
    )(x, mod, pos, freq, win_p, convw, lane_tab, qng, kvng, wuq_p, wukv_p)


def _gdn_kernel(q_ref, k_ref, v_ref, gb_ref, zs_ref, gn_ref, o_ref,
                state_ref, lhs_ref, sadd_ref, oadd_ref, egl_ref, *, n_chunks, group):
    c = GDN_CHUNK

    @pl.when(pl.program_id(1) == 0)
    def _():
        state_ref[...] = jnp.zeros(state_ref.shape, F32)

    row = lax.broadcasted_iota(jnp.int32, (c, c), 0)
    col = lax.broadcasted_iota(jnp.int32, (c, c), 1)
    incl = row >= col
    strict = row > col
    tri = incl.astype(BF16)
    eye = (row == col).astype(F32)
    gnorm = gn_ref[...]
    base_shift = int(math.log2(GDN_BASE))
    diag_mask = strict & ((row >> base_shift) == (col >> base_shift))
    merge_masks = []
    for sh in range(base_shift, int(math.log2(c))):
        merge_masks.append(((row >> sh) == (col >> sh) + 1) & (((row >> sh) & 1) == 1))
    heads = range(DN_HEADS)
    lanes = [slice(hd * DN_DK, (hd + 1) * DN_DK) for hd in heads]

    def prepare(gi, slot):
        sys = [(ch, hd) for ch in range(group) for hd in heads]
        r0s = [_aligned((gi * group + ch) * c, c) for ch in range(group)]
        gbts = [gb_ref[0, pl.ds(r0, c), :] for r0 in r0s]
        gcs, gcts = [], []
        for gbt in gbts:
            hi = gbt.astype(BF16)
            r1 = gbt - hi.astype(F32)
            mid = r1.astype(BF16)
            lo = (r1 - mid.astype(F32)).astype(BF16)
            gc_all = _dot(tri, hi) + _dot(tri, mid) + _dot(tri, lo)
            gcs.append(gc_all)
            gcts.append(gc_all.T)
        gcol = [gcs[ch][:, A_LANE + hd:A_LANE + hd + 1] for ch, hd in sys]
        grow = [gcts[ch][A_LANE + hd:A_LANE + hd + 1, :] for ch, hd in sys]
        beta = [gbts[ch][:, BETA_LANE + hd:BETA_LANE + hd + 1] for ch, hd in sys]
        glast = [gcs[ch][c - 1:c, A_LANE + hd:A_LANE + hd + 1] for ch, hd in sys]
        decay = [jnp.where(incl, jnp.exp(jnp.where(incl, a - b, 0.0)), 0.0) for a, b in zip(gcol, grow)]
        kb = [k_ref[0, pl.ds(r0s[ch], c), lanes[hd]] for ch, hd in sys]
        qb = [q_ref[0, pl.ds(r0s[ch], c), lanes[hd]] for ch, hd in sys]
        kq = [_dot_nt(jnp.concatenate([k, q], axis=0), k) for k, q in zip(kb, qb)]
        a_mat = [x[:c] * b * d for x, b, d in zip(kq, beta, decay)]
        attn = [(x[c:] * d).astype(BF16) for x, d in zip(kq, decay)]
        dmat = [jnp.where(diag_mask, a, 0.0) for a in a_mat]
        t = [eye - d for d in dmat]
        p = [(-d).astype(BF16) for d in dmat]
        p = [_dot(x, x) for x in p]
        for _ in range(base_shift - 2):
            pb = [x.astype(BF16) for x in p]
            tp = [_dot(jnp.concatenate([x.astype(BF16), y], axis=0), y) for x, y in zip(t, pb)]
            t = [x + y[:c] for x, y in zip(t, tp)]
            p = [y[c:] for y in tp]
        t = [x + _dot(x.astype(BF16), y.astype(BF16)) for x, y in zip(t, p)]
        for mask in merge_masks:
            tb = [x.astype(BF16) for x in t]
            lt = [_dot(jnp.where(mask, a, 0.0).astype(BF16), x) for a, x in zip(a_mat, tb)]
            t = [x - _dot(xb, y.astype(BF16)) for x, xb, y in zip(t, tb, lt)]
        egc = [jnp.exp(x) for x in gcol]
        kf = [x.astype(F32) for x in kb]
        vf = [v_ref[0, pl.ds(r0s[ch], c), lanes[hd]].astype(F32) for ch, hd in sys]
        rhs = [jnp.concatenate([(k * (b * e)).astype(BF16), (v * b).astype(BF16)], axis=1)
               for k, v, b, e in zip(kf, vf, beta, egc)]
        wu = [_dot(x.astype(BF16), r).astype(BF16) for x, r in zip(t, rhs)]
        qg = [q.astype(F32) * e for q, e in zip(qb, egc)]
        kdt = [(k * jnp.exp(gl - gc)).T.astype(BF16) for k, gl, gc in zip(kf, glast, gcol)]
        ku = [_dot(x, y) for x, y in zip(kdt, wu)]
        au = [_dot(x, y) for x, y in zip(attn, wu)]
        for i, (ch, hd) in enumerate(sys):
            sc = slot * group + ch
            lhs_ref[hd, sc * LHS_ROWS:sc * LHS_ROWS + DN_DK, :] = (-ku[i][:, :DN_DK]).astype(BF16)
            lhs_ref[hd, sc * LHS_ROWS + DN_DK:(sc + 1) * LHS_ROWS, :] = (qg[i] - au[i][:, :DN_DK]).astype(BF16)
            sadd_ref[hd, sc * DN_DK:(sc + 1) * DN_DK, :] = ku[i][:, DN_DK:]
            oadd_ref[hd, sc * c:(sc + 1) * c, :] = au[i][:, DN_DK:]
            egl_ref[sc * DN_HEADS + hd:sc * DN_HEADS + hd + 1, :] = jnp.broadcast_to(jnp.exp(glast[i]), (1, LANES))

    def recur(gi, slot):
        for ch in range(group):
            sc = slot * group + ch
            io_rows = pl.ds(_aligned((gi * group + ch) * c, c), c)
            state = [state_ref[hd] for hd in heads]
            xs = [_dot(lhs_ref[hd, sc * LHS_ROWS:(sc + 1) * LHS_ROWS, :], state[hd].astype(BF16)) for hd in heads]
            for hd in heads:
                egl = egl_ref[sc * DN_HEADS + hd:sc * DN_HEADS + hd + 1, :]
                state_ref[hd] = state[hd] * egl + xs[hd][:DN_DK] + sadd_ref[hd, sc * DN_DK:(sc + 1) * DN_DK, :]
            for hd in heads:
                o = xs[hd][DN_DK:] + oadd_ref[hd, sc * c:(sc + 1) * c, :]
                on = o * lax.rsqrt(jnp.mean(o * o, axis=-1, keepdims=True) + 1e-6) * gnorm
                o_ref[0, io_rows, lanes[hd]] = (on * zs_ref[0, io_rows, lanes[hd]].astype(F32)).astype(BF16)

    n_groups = n_chunks // group
    prepare(0, 0)

    def pair(t, carry):
        prepare(2 * t + 1, 1)
        recur(2 * t, 0)
        prepare(2 * t + 2, 0)
        recur(2 * t + 1, 1)
        return carry

    lax.fori_loop(0, n_groups // 2 - 1, pair, 0)
    prepare(n_groups - 1, 1)
    recur(n_groups - 2, 0)
    recur(n_groups - 1, 1)


def _gdn(q_dn, k_dn, v_dn, gb, zs, gnorm, *, tg, group):
    bsz, s, _ = q_dn.shape
    n_chunks = tg // GDN_CHUNK
    assert n_chunks % (2 * group) == 0
    slots = 2 * group
    row = lambda w: pl.BlockSpec((1, tg, w), lambda b, i: (b, i, 0))
    return pl.pallas_call(
        functools.partial(_gdn_kernel, n_chunks=n_chunks, group=group),
        grid=(bsz, s // tg),
        in_specs=[row(DN_QK), row(DN_QK), row(DN_VW), row(LANES), row(DN_VW), _const_spec(gnorm.shape)],
        out_specs=row(DN_VW),
        out_shape=jax.ShapeDtypeStruct((bsz, s, DN_VW), BF16),
        scratch_shapes=[pltpu.VMEM((DN_HEADS, DN_DK, DN_DV), F32),
                        pltpu.VMEM((DN_HEADS, slots * LHS_ROWS, DN_DK), BF16),
                        pltpu.VMEM((DN_HEADS, slots * DN_DK, DN_DV), F32),
                        pltpu.VMEM((DN_HEADS, slots * GDN_CHUNK, DN_DV), F32),
                        pltpu.VMEM((slots * DN_HEADS, LANES), F32)],
        compiler_params=pltpu.CompilerParams(dimension_semantics=("arbitrary", "arbitrary"),
                                             vmem_limit_bytes=VMEM_LIMIT),
        name="gdn",
    )(q_dn, k_dn, v_dn, gb, zs, gnorm)


NEG_BIG = -1e30
ATTN_HEADS_PER_STEP = 4
FFN_PIECES = 8
BF16_ROWS = 16
HUGE = 3.0e38


def _attn_kernel(q_ref, k_ref, vt_ref, o_ref, s_ref, m_ref, acc_ref, bias_ref, *, tq, nh):
    qi = pl.program_id(2)
    heads = range(nh)

    @pl.when((pl.program_id(0) == 0) & (pl.program_id(1) == 0) & (qi == 0))
    def _():
        kidx = lax.broadcasted_iota(jnp.int32, (tq, tq), 0)
        qidx = lax.broadcasted_iota(jnp.int32, (tq, tq), 1)
        bias_ref[...] = jnp.where(kidx <= qidx, 0.0, NEG_BIG)

    q = [q_ref[0, :, h * QATT_W:(h + 1) * QATT_W] for h in heads]
    m_ref[...] = jnp.full(m_ref.shape, NEG_BIG, F32)
    acc_ref[...] = jnp.zeros(acc_ref.shape, F32)
    ones = jnp.ones((BF16_ROWS, tq), BF16)

    def scores(j, h):
        rows = pl.ds(pl.multiple_of(j * tq, tq), tq)
        return _dot_nt(k_ref[0, rows, h * QATT_W:(h + 1) * QATT_W], q[h])

    def update(j, s):
        m_old = [m_ref[h] for h in heads]
        m_new = [jnp.maximum(m_old[h], jnp.max(s[h], axis=0, keepdims=True)) for h in heads]
        alpha = [jnp.exp2(m_old[h] - m_new[h]) for h in heads]
        p = [jnp.exp2(s[h] - m_new[h]) for h in heads]
        for h in heads:
            v1 = jnp.concatenate([vt_ref[0, h, j], ones], axis=0)
            acc_ref[h] = acc_ref[h] * alpha[h] + _dot(v1, p[h].astype(BF16))
            m_ref[h] = m_new[h]

    for h in heads:
        s_ref[h, 0] = scores(0, h)

    def step(j, slot):
        for h in heads:
            s_ref[h, 1 - slot] = scores(j + 1, h)
        update(j, [s_ref[h, slot] for h in heads])

    def body(t, carry):
        step(2 * t, 0)
        step(2 * t + 1, 1)
        return carry

    lax.fori_loop(0, qi // 2, body, 0)

    @pl.when(qi % 2 == 1)
    def _():
        step(qi - 1, 0)

    bias = bias_ref[...]
    update(qi, [s_ref[h, qi & 1] + bias for h in heads])
    for h in heads:
        acc = acc_ref[h]
        o_ref[0, :, h * V_HEAD:(h + 1) * V_HEAD] = (acc[:V_HEAD] / acc[V_HEAD:V_HEAD + 1]).T.astype(BF16)


def _attention(q_att, k_att, vt, *, tq, nh):
    bsz, s, _ = q_att.shape
    nblk = s // tq
    return pl.pallas_call(
        functools.partial(_attn_kernel, tq=tq, nh=nh),
        grid=(bsz, MLA_HEADS // nh, nblk),
        in_specs=[pl.BlockSpec((1, tq, nh * QATT_W), lambda b, h, i: (b, i, h)),
                  pl.BlockSpec((1, s, nh * QATT_W), lambda b, h, i: (b, 0, h)),
                  pl.BlockSpec((1, nh, nblk, V_HEAD, tq), lambda b, h, i: (b, h, 0, 0, 0))],
        out_specs=pl.BlockSpec((1, tq, nh * V_HEAD), lambda b, h, i: (b, i, h)),
        out_shape=jax.ShapeDtypeStruct((bsz, s, MLA_HEADS * V_HEAD), BF16),
        scratch_shapes=[pltpu.VMEM((nh, 2, tq, tq), F32),
                        pltpu.VMEM((nh, 1, tq), F32),
                        pltpu.VMEM((nh, V_HEAD + BF16_ROWS, tq), F32),
                        pltpu.VMEM((tq, tq), F32)],
        compiler_params=pltpu.CompilerParams(dimension_semantics=("arbitrary", "arbitrary", "arbitrary"),
                                             vmem_limit_bytes=VMEM_LIMIT),
        name="mla_attention",
    )(q_att, k_att, vt)


def _layernorm(y, g, b):
    mu = jnp.mean(y, axis=-1, keepdims=True)
    yc = y - mu
    var = jnp.mean(yc * yc, axis=-1, keepdims=True)
    return yc * lax.rsqrt(var + 1e-5) * g + b


def _out_kernel(x_ref, og_ref, om_ref, mod_in_ref, mod_out_ref, wo_ref, ln_ref, wg_ref, wu_ref, wd_ref, o_ref,
                act_ref, x1_ref, h_ref, ff_ref, *, alpha, ff_chunk, n_blocks):
    n = pl.program_id(0)
    nw = og_ref.shape[-1]
    d_ff = wg_ref.shape[1]

    tm = x_ref.shape[1]

    def head_rows(slot, rows, mix):
        gt_m = mod_in_ref[0, 2:3, :]
        sh_f = mod_in_ref[0, 3:4, :]
        sc_f = mod_in_ref[0, 4:5, :]
        x1 = _layernorm(alpha * x_ref[0, rows, :] + gt_m * mix, ln_ref[0:1, :], ln_ref[1:2, :])
        h = (x1 * (1.0 + sc_f) + sh_f).astype(BF16)
        x1_ref[slot, rows, :] = x1
        h_ref[slot, rows, :] = h
        return h

    def mixer_proj():
        return _dot(og_ref[0], wo_ref[0:nw, :]) + _dot(om_ref[0], wo_ref[nw:, :])

    def head(slot):
        head_rows(slot, slice(0, tm), mixer_proj())

    def tail_rows(slot, rows):
        gt_f = mod_out_ref[0, 5:6, :]
        o = _layernorm(alpha * x1_ref[slot, rows, :] + gt_f * ff_ref[slot, rows, :], ln_ref[2:3, :], ln_ref[3:4, :])
        o_ref[0, rows, :] = o
        return o

    def tail(slot):
        tail_rows(slot, slice(0, tm))

    def ffn(slot, gates=None):
        for k, lo in enumerate(range(0, d_ff, ff_chunk)):
            h = h_ref[slot]
            if gates is not None and gates[k] is not None:
                h = jnp.concatenate([h[:BF16_ROWS] + gates[k], h[BF16_ROWS:]], axis=0)
            g = _dot(h, wg_ref[:, lo:lo + ff_chunk])
            u = _dot(h, wu_ref[:, lo:lo + ff_chunk])
            act_ref[:, lo:lo + ff_chunk] = (_silu(g) * u).astype(BF16)
        ff_ref[slot] = _dot(act_ref[...], wd_ref[...])

    def zero_after(*values):
        flag = None
        for v in values:
            v = v.astype(F32)
            top = v[0:SUBLANES]
            for r in range(SUBLANES, v.shape[0], SUBLANES):
                top = jnp.maximum(top, v[r:r + SUBLANES])
            over = top > HUGE
            flag = over if flag is None else (flag | over)
        z = jnp.where(flag, 1.0, 0.0)
        return jnp.concatenate([z, z], axis=0).astype(BF16)

    @pl.when(n == 0)
    def _():
        x1_ref[1] = jnp.zeros(x1_ref.shape[1:], F32)
        ff_ref[1] = jnp.zeros(ff_ref.shape[1:], F32)
        head(0)

    def steady(p):
        n_chunks = -(-d_ff // ff_chunk)
        mix = mixer_proj()
        gates = [None] * n_chunks
        piece = tm // FFN_PIECES
        for k in range(FFN_PIECES):
            rows = slice(k * piece, (k + 1) * piece)
            o = tail_rows(p, rows)
            h = head_rows(p, rows, mix[k * piece:(k + 1) * piece])
            gates[k + n_chunks - FFN_PIECES - 1] = zero_after(o, h)
        ffn(1 - p, gates)

    @pl.when((n >= 1) & (n <= n_blocks) & (n % 2 == 0))
    def _():
        steady(0)

    @pl.when((n >= 1) & (n <= n_blocks) & (n % 2 == 1))
    def _():
        steady(1)

    @pl.when(n == n_blocks + 1)
    def _():
        tail((n_blocks + 1) % 2)


def _out_ffn(x, og, om, mod, wo, ln, wg, wu, wd, *, tm, alpha, ff_chunk):
    bsz, s, d = x.shape
    d_ff = wg.shape[1]
    nb = s // tm
    n_blocks = bsz * nb
    blk_in = lambda n: jnp.minimum(n, n_blocks - 1)
    blk_out = lambda n: jnp.clip(n - 2, 0, n_blocks - 1)
    row_in = lambda w: pl.BlockSpec((1, tm, w), lambda n: (blk_in(n) // nb, blk_in(n) % nb, 0))
    mod_spec = lambda blk: pl.BlockSpec((1, mod.shape[1], d), lambda n: (blk(n) // nb, 0, 0))
    return pl.pallas_call(
        functools.partial(_out_kernel, alpha=alpha, ff_chunk=ff_chunk, n_blocks=n_blocks),
        grid=(n_blocks + 2,),
        in_specs=[row_in(d), row_in(og.shape[-1]), row_in(om.shape[-1]),
                  mod_spec(blk_in), mod_spec(blk_out),
                  _const_spec(wo.shape), _const_spec(ln.shape),
                  _const_spec(wg.shape), _const_spec(wu.shape), _const_spec(wd.shape)],
        out_specs=pl.BlockSpec((1, tm, d), lambda n: (blk_out(n) // nb, blk_out(n) % nb, 0)),
        out_shape=jax.ShapeDtypeStruct((bsz, s, d), F32),
        scratch_shapes=[pltpu.VMEM((tm, d_ff), BF16),
                        pltpu.VMEM((2, tm, d), F32),
                        pltpu.VMEM((2, tm, d), BF16),
                        pltpu.VMEM((2, tm, d), F32)],
        compiler_params=pltpu.CompilerParams(dimension_semantics=("arbitrary",),
                                             vmem_limit_bytes=VMEM_LIMIT),
        name="out_ffn",
    )(x, og, om, mod, mod, wo, ln, wg, wu, wd)


def _pack_w_in(w_in):
    d = w_in.shape[0]
    split_z = DN_CONV_CH
    split_beta = split_z + DN_VW
    split_a = split_beta + DN_HEADS
    split_cq = split_a + DN_HEADS
    split_ckv = split_cq + Q_LORA
    split_kr = split_ckv + KV_LORA
    kr = w_in[:, split_kr:split_kr + QK_ROPE]
    zeros = lambda n: jnp.zeros((d, n), w_in.dtype)
    misc = jnp.concatenate([
        kr[:, :HALF_ROPE], zeros(KR2_LANE - HALF_ROPE),
        kr[:, HALF_ROPE:], zeros(BETA_LANE - KR2_LANE - HALF_ROPE),
        w_in[:, split_beta:split_a], w_in[:, split_a:split_cq],
        zeros(LANES - A_LANE - DN_HEADS)], axis=1)
    return jnp.concatenate([w_in[:, :split_beta], w_in[:, split_cq:split_kr], misc], axis=1).astype(BF16)


def _pack_w_uq(w_uq):
    q_lora = w_uq.shape[0]
    w = w_uq.reshape(q_lora, MLA_HEADS, QK_NOPE + QK_ROPE)
    zeros = lambda n: jnp.zeros((q_lora, MLA_HEADS, n), w_uq.dtype)
    tile = jnp.concatenate([
        w[:, :, :QK_NOPE],
        w[:, :, QK_NOPE:QK_NOPE + HALF_ROPE], zeros(KR2_LANE - HALF_ROPE),
        w[:, :, QK_NOPE + HALF_ROPE:], zeros(LANES - KR2_LANE - HALF_ROPE)], axis=2)
    return tile.reshape(q_lora, MLA_HEADS * QATT_W).astype(BF16)


def _pack_w_ukv(w_ukv):
    kv_lora = w_ukv.shape[0]
    w = w_ukv.reshape(kv_lora, MLA_HEADS, QK_NOPE + V_HEAD)
    return jnp.concatenate([w[:, :, :QK_NOPE].reshape(kv_lora, -1),
                            w[:, :, QK_NOPE:].reshape(kv_lora, -1)], axis=1).astype(BF16)


def _lane_table(a_log, dt_bias):
    z = jnp.zeros((LANES,), F32)
    put = lambda vec, lane: z.at[lane:lane + vec.shape[0]].set(vec)
    rows = [put(dt_bias.astype(F32), A_LANE), put(a_log.astype(F32), A_LANE)]
    rows += [z] * (SUBLANES - len(rows))
    return jnp.stack(rows)


def _inv_freq_column():
    inv_freq = 1.0 / (ROPE_THETA ** (jnp.arange(0, QK_ROPE, 2, dtype=F32) / QK_ROPE))
    return inv_freq.reshape(HALF_ROPE, 1)


def _layer(x, c_mod, pos, w_in, conv_w, a_log, dt_bias, dn_norm_g, q_norm_g, w_uq, kv_norm_g, w_ukv,
           w_o, ln1_g, ln1_b, w_gate, w_up, w_down, ln2_g, ln2_b, *, depth, tm, tg, gdn_group, tq, ff_chunk):
    alpha = (2.0 * depth) ** 0.25
    q_dn, k_dn, v_dn, zs, gb, q_att, k_att, vt = _inproj(
        x, c_mod, pos, _inv_freq_column(), _pack_w_in(w_in), conv_w.reshape(CONV_K, DN_CONV_CH).astype(F32),
        _lane_table(a_log, dt_bias), q_norm_g.reshape(1, -1).astype(F32), kv_norm_g.reshape(1, -1).astype(F32),
        _pack_w_uq(w_uq), _pack_w_ukv(w_ukv), tm=tq)
    og = _gdn(q_dn, k_dn, v_dn, gb, zs, dn_norm_g.reshape(1, -1).astype(F32), tg=tg, group=gdn_group)
    om = _attention(q_att, k_att, vt, tq=tq, nh=ATTN_HEADS_PER_STEP)
    ln = jnp.stack([ln1_g, ln1_b, ln2_g, ln2_b]).astype(F32)
    return _out_ffn(x, og, om, c_mod, w_o.astype(BF16), ln, w_gate.astype(BF16), w_up.astype(BF16),
                    w_down.astype(BF16), tm=tm, alpha=alpha, ff_chunk=ff_chunk)


def kernel(x, c, positions, w_ada, b_ada, w_in, conv_w, a_log, dt_bias, dn_norm_g, q_norm_g, w_uq, kv_norm_g, w_ukv, w_o, ln1_g, ln1_b, w_gate, w_up, w_down, ln2_g, ln2_b):
    bsz, s, d = x.shape
    depth = w_in.shape[0]
    tile = min(512, s)
    pos = positions.astype(F32).reshape(bsz, 1, s)
    for l in range(depth):
        mod = _modulation(c, w_ada[l], b_ada[l]).reshape(bsz, 6, d)
        x = _layer(x, mod, pos, w_in[l], conv_w[l], a_log[l], dt_bias[l], dn_norm_g[l], q_norm_g[l], w_uq[l],
                   kv_norm_g[l], w_ukv[l], w_o[l], ln1_g[l], ln1_b[l], w_gate[l], w_up[l], w_down[l],
                   ln2_g[l], ln2_b[l], depth=depth, tm=tile, tg=min(GDN_TILE, s), gdn_group=GDN_GROUP, tq=tile, ff_chunk=256)
    return x
```

```python
import functools
import math

import jax
import jax.numpy as jnp
from jax import lax
from jax.experimental import pallas as pl
from jax.experimental.pallas import tpu as pltpu

F32 = jnp.float32
BF16 = jnp.bfloat16

DN_HEADS = 4
DN_DK = 128
DN_DV = 128
CONV_K = 4
MLA_HEADS = 4
QK_NOPE = 128
QK_ROPE = 64
V_HEAD = 128
Q_LORA = 512
KV_LORA = 256
ROPE_THETA = 10000.0

DN_QK = DN_HEADS * DN_DK
DN_VW = DN_HEADS * DN_DV
DN_CONV_CH = 2 * DN_QK + DN_VW

LANES = 128
SUBLANES = 8
MXU_COLS = 256
GDN_CHUNK = 64
GDN_BASE = 8
LHS_ROWS = DN_DK + GDN_CHUNK
GDN_TILE = 1024
GDN_GROUP = 4
VMEM_LIMIT = 56 * 1024 * 1024

HALF_ROPE = QK_ROPE // 2
KR1_LANE = 0
KR2_LANE = 64
BETA_LANE = 96
A_LANE = 100
COL_QKV = 0
COL_Z = DN_CONV_CH
N_IN_MAIN = COL_Z + DN_VW
COL_CQ = 0
COL_CKV = COL_CQ + Q_LORA
COL_MISC = COL_CKV + KV_LORA
N_IN_TAIL = COL_MISC + LANES
QATT_W = 2 * LANES


def _dot(a, b):
    return jnp.dot(a, b, preferred_element_type=F32)


def _dot_f32(a, b):
    return jnp.dot(a, b, preferred_element_type=F32, precision=lax.Precision.HIGHEST)


def _dot_nt(a, b):
    return lax.dot_general(a, b, (((1,), (1,)), ((), ())), preferred_element_type=F32)


def _sigmoid(x):
    return 0.5 + 0.5 * jnp.tanh(0.5 * x)


def _silu(x):
    half = 0.5 * x
    return half + half * jnp.tanh(half)


def _aligned(start, multiple):
    return start if isinstance(start, int) else pl.multiple_of(start, multiple)


def _const_spec(shape):
    zeros = (0,) * len(shape)
    return pl.BlockSpec(shape, lambda *_: zeros, pipeline_mode=pl.Buffered(1))


def _mod_kernel(c_ref, w_ref, b_ref, o_ref):
    o_ref[...] = _dot(_silu(c_ref[...]), w_ref[...]) + b_ref[...]


def _modulation(c, w_ada, b_ada):
    bsz, d = c.shape
    n = w_ada.shape[1]
    return pl.pallas_call(
        _mod_kernel,
        grid=(n // d,),
        in_specs=[pl.BlockSpec((bsz, d), lambda j: (0, 0)),
                  pl.BlockSpec((d, d), lambda j: (0, j)),
                  pl.BlockSpec((1, d), lambda j: (0, j))],
        out_specs=pl.BlockSpec((bsz, d), lambda j: (0, j)),
        out_shape=jax.ShapeDtypeStruct((bsz, n), F32),
        compiler_params=pltpu.CompilerParams(dimension_semantics=("arbitrary",),
                                             vmem_limit_bytes=VMEM_LIMIT),
        name="modulation",
    )(c, w_ada, b_ada.reshape(1, n))


def _inproj_kernel(x_ref, mod_ref, pos_ref, freq_ref, win_ref, wtail_ref, convw_ref, lane_ref, qng_ref, kvng_ref,
                   wuq_ref, wukv_ref,
                   qdn_ref, kdn_ref, vdn_ref, zs_ref, gb_ref, qatt_ref, katt_ref, vt_ref,
                   halo_ref, *, tm):
    i = pl.program_id(1)
    sh_m = mod_ref[0, 0:1, :]
    sc_m = mod_ref[0, 1:2, :]
    dt_bias = lane_ref[0:1, :]
    neg_a = -jnp.exp(lane_ref[1:2, :])
    scale =math.log2(math.e) / math.sqrt(QK_NOPE + QK_ROPE)

    @pl.when(i == 0)
    def _():
        halo_ref[0:SUBLANES, :] = jnp.zeros((SUBLANES, DN_CONV_CH), F32)

    def rows_pass(r0, nr):
        rows = slice(r0, r0 + nr)
        hrows = slice(SUBLANES + r0, SUBLANES + r0 + nr)
        h = (x_ref[0, rows, :] * (1.0 + sc_m) + sh_m).astype(BF16)

        for lo in range(0, DN_CONV_CH, MXU_COLS):
            cols = slice(lo, lo + MXU_COLS)
            halo_ref[hrows, cols] = _dot(h, win_ref[:, COL_QKV + lo:COL_QKV + lo + MXU_COLS])
            conv = halo_ref[hrows, cols] * convw_ref[CONV_K - 1:CONV_K, cols]
            for d in range(1, CONV_K):
                shifted = halo_ref[SUBLANES + r0 - d:SUBLANES + r0 - d + nr, cols]
                conv = conv + shifted * convw_ref[CONV_K - 1 - d:CONV_K - d, cols]
            act = _silu(conv)
            if lo >= 2 * DN_QK:
                vdn_ref[0, rows, lo - 2 * DN_QK:lo - 2 * DN_QK + MXU_COLS] = act.astype(BF16)
                continue
            dst, off, gain = (qdn_ref, lo, DN_DK ** -0.5) if lo < DN_QK else (kdn_ref, lo - DN_QK, 1.0)
            for sub in range(0, MXU_COLS, DN_DK):
                t = act[:, sub:sub + DN_DK]
                t = t * (lax.rsqrt(jnp.sum(t * t, axis=-1, keepdims=True) + 1e-6) * gain)
                dst[0, rows, off + sub:off + sub + DN_DK] = t.astype(BF16)

        for lo in range(0, DN_VW, MXU_COLS):
            zs_ref[0, rows, lo:lo + MXU_COLS] = _silu(
                _dot(h, win_ref[:, COL_Z + lo:COL_Z + lo + MXU_COLS])).astype(BF16)

        ang = freq_ref[...] * pos_ref[0, :, rows]
        cos_f = jnp.cos(ang)
        sin_f = jnp.sin(ang)
        gap1 = jnp.zeros((KR2_LANE - HALF_ROPE, nr), F32)
        gap2 = jnp.zeros((LANES - KR2_LANE - HALF_ROPE, nr), F32)
        cos_t = jnp.concatenate([cos_f, gap1, cos_f, gap2], axis=0).T
        sin_t = jnp.concatenate([-sin_f, gap1, sin_f, gap2], axis=0).T

        def rope(t):
            return t * cos_t + pltpu.roll(t, KR2_LANE - KR1_LANE, 1) * sin_t

        misc = _dot(h, wtail_ref[:, COL_MISC:COL_MISC + LANES])
        k_rope = rope(misc).astype(BF16)
        lane = lax.broadcasted_iota(jnp.int32, misc.shape, 1)
        a_in = misc + dt_bias
        softplus = jnp.maximum(a_in, 0.0) + jnp.log(1.0 + jnp.exp(-jnp.abs(a_in)))
        gb_ref[0, rows, :] = jnp.where(
            (lane >= BETA_LANE) & (lane < BETA_LANE + DN_HEADS), _sigmoid(misc),
            jnp.where((lane >= A_LANE) & (lane < A_LANE + DN_HEADS), neg_a * softplus, 0.0))

        cq = _dot(h, wtail_ref[:, COL_CQ:COL_CQ + Q_LORA])
        cq = cq * lax.rsqrt(jnp.mean(cq * cq, axis=-1, keepdims=True) + 1e-6) * (qng_ref[...] * scale)
        qm = _dot(cq.astype(BF16), wuq_ref[...])
        for hd in range(MLA_HEADS):
            lo = hd * QATT_W
            qatt_ref[0, rows, lo:lo + LANES] = qm[:, lo:lo + LANES].astype(BF16)
            qatt_ref[0, rows, lo + LANES:lo + QATT_W] = rope(qm[:, lo + LANES:lo + QATT_W]).astype(BF16)

        ckv = _dot(h, wtail_ref[:, COL_CKV:COL_CKV + KV_LORA])
        ckv = ckv * lax.rsqrt(jnp.mean(ckv * ckv, axis=-1, keepdims=True) + 1e-6) * kvng_ref[...]
        kv = _dot(ckv.astype(BF16), wukv_ref[...])
        for hd in range(MLA_HEADS):
            lo = hd * QATT_W
            katt_ref[0, rows, lo:lo + LANES] = kv[:, hd * QK_NOPE:(hd + 1) * QK_NOPE].astype(BF16)
            katt_ref[0, rows, lo + LANES:lo + QATT_W] = k_rope
            v = kv[:, MLA_HEADS * QK_NOPE + hd * V_HEAD:MLA_HEADS * QK_NOPE + (hd + 1) * V_HEAD]
            vt_ref[0, hd, 0, :, rows] = v.T.astype(BF16)

    rows_pass(0, tm)
    halo_ref[0:SUBLANES, :] = halo_ref[tm:tm + SUBLANES, :]


def _inproj(x, mod, pos, freq, win_b, wtail, convw, lane_tab, qng, kvng, wuq_p, wukv_p, *, tm):
    bsz, s, d = x.shape
    nblk = s // tm
    row = lambda w: pl.BlockSpec((1, tm, w), lambda b, i: (b, i, 0))
    out_shape = (
        jax.ShapeDtypeStruct((bsz, s, DN_QK), BF16),
        jax.ShapeDtypeStruct((bsz, s, DN_QK), BF16),
        jax.ShapeDtypeStruct((bsz, s, DN_VW), BF16),
        jax.ShapeDtypeStruct((bsz, s, DN_VW), BF16),
        jax.ShapeDtypeStruct((bsz, s, LANES), F32),
        jax.ShapeDtypeStruct((bsz, s, MLA_HEADS * QATT_W), BF16),
        jax.ShapeDtypeStruct((bsz, s, MLA_HEADS * QATT_W), BF16),
        jax.ShapeDtypeStruct((bsz, MLA_HEADS, nblk, V_HEAD, tm), BF16),
    )
    out_specs = (row(DN_QK), row(DN_QK), row(DN_VW), row(DN_VW), row(LANES),
                 row(MLA_HEADS * QATT_W), row(MLA_HEADS * QATT_W),
                 pl.BlockSpec((1, MLA_HEADS, 1, V_HEAD, tm), lambda b, i: (b, 0, i, 0, 0)))
    return pl.pallas_call(
        functools.partial(_inproj_kernel, tm=tm),
        grid=(bsz, nblk),
        in_specs=[row(d),
                  pl.BlockSpec((1, mod.shape[1], d), lambda b, i: (b, 0, 0)),
                  pl.BlockSpec((1, 1, tm), lambda b, i: (b, 0, i)),
                  _const_spec(freq.shape), _const_spec((d, N_IN_MAIN)), _const_spec(wtail.shape),
                  _const_spec(convw.shape), _const_spec(lane_tab.shape),
                  _const_spec(qng.shape), _const_spec(kvng.shape),
                  _const_spec(wuq_p.shape), _const_spec(wukv_p.shape)],
        out_specs=out_specs,
        out_shape=out_shape,
        scratch_shapes=[pltpu.VMEM((tm + 2 * SUBLANES, DN_CONV_CH), F32)],
        compiler_params=pltpu.CompilerParams(dimension_semantics=("arbitrary", "arbitrary"),
                                             vmem_limit_bytes=VMEM_LIMIT),
        name="inproj",
    )(x, mod, pos, freq, win_b, wtail, convw, lane_tab, qng, kvng, wuq_p, wukv_p)


def _gdn_kernel(q_ref, k_ref, v_ref, gb_ref, zs_ref, gn_ref, o_ref,
                state_ref, lhs_ref, sadd_ref, oadd_ref, egl_ref, *, n_chunks, group):
    c = GDN_CHUNK

    @pl.when(pl.program_id(1) == 0)
    def _():
        state_ref[...] = jnp.zeros(state_ref.shape, F32)

    row = lax.broadcasted_iota(jnp.int32, (c, c), 0)
    col = lax.broadcasted_iota(jnp.int32, (c, c), 1)
    incl = row >= col
    strict = row > col
    tri = incl.astype(BF16)
    eye = (row == col).astype(F32)
    gnorm = gn_ref[...]
    base_shift = int(math.log2(GDN_BASE))
    diag_mask = strict & ((row >> base_shift) == (col >> base_shift))
    merge_masks = []
    for sh in range(base_shift, int(math.log2(c))):
        merge_masks.append(((row >> sh) == (col >> sh) + 1) & (((row >> sh) & 1) == 1))
    heads = range(DN_HEADS)
    lanes = [slice(hd * DN_DK, (hd + 1) * DN_DK) for hd in heads]

    def prepare(gi, slot):
        sys = [(ch, hd) for ch in range(group) for hd in heads]
        r0s = [_aligned((gi * group + ch) * c, c) for ch in range(group)]
        gbts = [gb_ref[0, pl.ds(r0, c), :] for r0 in r0s]
        gcs, gcts = [], []
        for gbt in gbts:
            hi = gbt.astype(BF16)
            r1 = gbt - hi.astype(F32)
            mid = r1.astype(BF16)
            lo = (r1 - mid.astype(F32)).astype(BF16)
            gc_all = _dot(tri, hi) + _dot(tri, mid) + _dot(tri, lo)
            gcs.append(gc_all)
            gcts.append(gc_all.T)
        gcol = [gcs[ch][:, A_LANE + hd:A_LANE + hd + 1] for ch, hd in sys]
        grow = [gcts[ch][A_LANE + hd:A_LANE + hd + 1, :] for ch, hd in sys]
        beta = [gbts[ch][:, BETA_LANE + hd:BETA_LANE + hd + 1] for ch, hd in sys]
        glast = [gcs[ch][c - 1:c, A_LANE + hd:A_LANE + hd + 1] for ch, hd in sys]
        decay = [jnp.where(incl, jnp.exp(jnp.where(incl, a - b, 0.0)), 0.0) for a, b in zip(gcol, grow)]
        kb = [k_ref[0, pl.ds(r0s[ch], c), lanes[hd]] for ch, hd in sys]
        qb = [q_ref[0, pl.ds(r0s[ch], c), lanes[hd]] for ch, hd in sys]
        kq = [_dot_nt(jnp.concatenate([k, q], axis=0), k) for k, q in zip(kb, qb)]
        a_mat = [x[:c] * b * d for x, b, d in zip(kq, beta, decay)]
        attn = [(x[c:] * d).astype(BF16) for x, d in zip(kq, decay)]
        dmat = [jnp.where(diag_mask, a, 0.0) for a in a_mat]
        t = [eye - d for d in dmat]
        p = [(-d).astype(BF16) for d in dmat]
        p = [_dot(x, x) for x in p]
        for _ in range(base_shift - 2):
            pb = [x.astype(BF16) for x in p]
            tp = [_dot(jnp.concatenate([x.astype(BF16), y], axis=0), y) for x, y in zip(t, pb)]
            t = [x + y[:c] for x, y in zip(t, tp)]
            p = [y[c:] for y in tp]
        t = [x + _dot(x.astype(BF16), y.astype(BF16)) for x, y in zip(t, p)]
        for mask in merge_masks:
            tb = [x.astype(BF16) for x in t]
            lt = [_dot(jnp.where(mask, a, 0.0).astype(BF16), x) for a, x in zip(a_mat, tb)]
            t = [x - _dot(xb, y.astype(BF16)) for x, xb, y in zip(t, tb, lt)]
        egc = [jnp.exp(x) for x in gcol]
        kf = [x.astype(F32) for x in kb]
        vf = [v_ref[0, pl.ds(r0s[ch], c), lanes[hd]].astype(F32) for ch, hd in sys]
        rhs = [jnp.concatenate([(k * (b * e)).astype(BF16), (v * b).astype(BF16)], axis=1)
               for k, v, b, e in zip(kf, vf, beta, egc)]
        wu = [_dot(x.astype(BF16), r).astype(BF16) for x, r in zip(t, rhs)]
        qg = [q.astype(F32) * e for q, e in zip(qb, egc)]
        kdt = [(k * jnp.exp(gl - gc)).T.astype(BF16) for k, gl, gc in zip(kf, glast, gcol)]
        ku = [_dot(x, y) for x, y in zip(kdt, wu)]
        au = [_dot(x, y) for x, y in zip(attn, wu)]
        for i, (ch, hd) in enumerate(sys):
            sc = slot * group + ch
            lhs_ref[hd, sc * LHS_ROWS:sc * LHS_ROWS + DN_DK, :] = (-ku[i][:, :DN_DK]).astype(BF16)
            lhs_ref[hd, sc * LHS_ROWS + DN_DK:(sc + 1) * LHS_ROWS, :] = (qg[i] - au[i][:, :DN_DK]).astype(BF16)
            sadd_ref[hd, sc * DN_DK:(sc + 1) * DN_DK, :] = ku[i][:, DN_DK:]
            oadd_ref[hd, sc * c:(sc + 1) * c, :] = au[i][:, DN_DK:]
            egl_ref[sc * DN_HEADS + hd:sc * DN_HEADS + hd + 1, :] = jnp.broadcast_to(jnp.exp(glast[i]), (1, LANES))

    def recur(gi, slot):
        for ch in range(group):
            sc = slot * group + ch
            io_rows = pl.ds(_aligned((gi * group + ch) * c, c), c)
            state = [state_ref[hd] for hd in heads]
            xs = [_dot(lhs_ref[hd, sc * LHS_ROWS:(sc + 1) * LHS_ROWS, :], state[hd].astype(BF16)) for hd in heads]
            for hd in heads:
                egl = egl_ref[sc * DN_HEADS + hd:sc * DN_HEADS + hd + 1, :]
                state_ref[hd] = state[hd] * egl + xs[hd][:DN_DK] + sadd_ref[hd, sc * DN_DK:(sc + 1) * DN_DK, :]
            for hd in heads:
                o = xs[hd][DN_DK:] + oadd_ref[hd, sc * c:(sc + 1) * c, :]
                on = o * lax.rsqrt(jnp.mean(o * o, axis=-1, keepdims=True) + 1e-6) * gnorm
                o_ref[0, io_rows, lanes[hd]] = (on * zs_ref[0, io_rows, lanes[hd]].astype(F32)).astype(BF16)

    n_groups = n_chunks // group
    prepare(0, 0)

    def pair(t, carry):
        prepare(2 * t + 1, 1)
        recur(2 * t, 0)
        prepare(2 * t + 2, 0)
        recur(2 * t + 1, 1)
        return carry

    lax.fori_loop(0, n_groups // 2 - 1, pair, 0)
    prepare(n_groups - 1, 1)
    recur(n_groups - 2, 0)
    recur(n_groups - 1, 1)


def _gdn(q_dn, k_dn, v_dn, gb, zs, gnorm, *, tg, group):
    bsz, s, _ = q_dn.shape
    n_chunks = tg // GDN_CHUNK
    assert n_chunks % (2 * group) == 0
    slots = 2 * group
    row = lambda w: pl.BlockSpec((1, tg, w), lambda b, i: (b, i, 0))
    return pl.pallas_call(
        functools.partial(_gdn_kernel, n_chunks=n_chunks, group=group),
        grid=(bsz, s // tg),
        in_specs=[row(DN_QK), row(DN_QK), row(DN_VW), row(LANES), row(DN_VW), _const_spec(gnorm.shape)],
        out_specs=row(DN_VW),
        out_shape=jax.ShapeDtypeStruct((bsz, s, DN_VW), BF16),
        scratch_shapes=[pltpu.VMEM((DN_HEADS, DN_DK, DN_DV), F32),
                        pltpu.VMEM((DN_HEADS, slots * LHS_ROWS, DN_DK), BF16),
                        pltpu.VMEM((DN_HEADS, slots * DN_DK, DN_DV), F32),
                        pltpu.VMEM((DN_HEADS, slots * GDN_CHUNK, DN_DV), F32),
                        pltpu.VMEM((slots * DN_HEADS, LANES), F32)],
        compiler_params=pltpu.CompilerParams(dimension_semantics=("arbitrary", "arbitrary"),
                                             vmem_limit_bytes=VMEM_LIMIT),
        name="gdn",
    )(q_dn, k_dn, v_dn, gb, zs, gnorm)


NEG_BIG = -1e30
ATTN_HEADS_PER_STEP = 4
FFN_PIECES = 8
BF16_ROWS = 16
HUGE = 3.0e38


def _attn_kernel(q_ref, k_ref, vt_ref, o_ref, s_ref, m_ref, acc_ref, bias_ref, *, tq, nh):
    qi = pl.program_id(2)
    heads = range(nh)

    @pl.when((pl.program_id(0) == 0) & (pl.program_id(1) == 0) & (qi == 0))
    def _():
        kidx = lax.broadcasted_iota(jnp.int32, (tq, tq), 0)
        qidx = lax.broadcasted_iota(jnp.int32, (tq, tq), 1)
        bias_ref[...] = jnp.where(kidx <= qidx, 0.0, NEG_BIG)

    q = [q_ref[0, :, h * QATT_W:(h + 1) * QATT_W] for h in heads]
    m_ref[...] = jnp.full(m_ref.shape, NEG_BIG, F32)
    acc_ref[...] = jnp.zeros(acc_ref.shape, F32)
    ones = jnp.ones((BF16_ROWS, tq), BF16)

    def scores(j, h):
        rows = pl.ds(pl.multiple_of(j * tq, tq), tq)
        return _dot_nt(k_ref[0, rows, h * QATT_W:(h + 1) * QATT_W], q[h])

    def update(j, s):
        m_old = [m_ref[h] for h in heads]
        m_new = [jnp.maximum(m_old[h], jnp.max(s[h], axis=0, keepdims=True)) for h in heads]
        alpha = [jnp.exp2(m_old[h] - m_new[h]) for h in heads]
        p = [jnp.exp2(s[h] - m_new[h]) for h in heads]
        for h in heads:
            v1 = jnp.concatenate([vt_ref[0, h, j], ones], axis=0)
            acc_ref[h] = acc_ref[h] * alpha[h] + _dot(v1, p[h].astype(BF16))
            m_ref[h] = m_new[h]

    for h in heads:
        s_ref[h, 0] = scores(0, h)

    def step(j, slot):
        for h in heads:
            s_ref[h, 1 - slot] = scores(j + 1, h)
        update(j, [s_ref[h, slot] for h in heads])

    def body(t, carry):
        step(2 * t, 0)
        step(2 * t + 1, 1)
        return carry

    lax.fori_loop(0, qi // 2, body, 0)

    @pl.when(qi % 2 == 1)
    def _():
        step(qi - 1, 0)

    bias = bias_ref[...]
    update(qi, [s_ref[h, qi & 1] + bias for h in heads])
    for h in heads:
        acc = acc_ref[h]
        o_ref[0, :, h * V_HEAD:(h + 1) * V_HEAD] = (acc[:V_HEAD] / acc[V_HEAD:V_HEAD + 1]).T.astype(BF16)


def _attention(q_att, k_att, vt, *, tq, nh):
    bsz, s, _ = q_att.shape
    nblk = s // tq
    return pl.pallas_call(
        functools.partial(_attn_kernel, tq=tq, nh=nh),
        grid=(bsz, MLA_HEADS // nh, nblk),
        in_specs=[pl.BlockSpec((1, tq, nh * QATT_W), lambda b, h, i: (b, i, h)),
                  pl.BlockSpec((1, s, nh * QATT_W), lambda b, h, i: (b, 0, h)),
                  pl.BlockSpec((1, nh, nblk, V_HEAD, tq), lambda b, h, i: (b, h, 0, 0, 0))],
        out_specs=pl.BlockSpec((1, tq, nh * V_HEAD), lambda b, h, i: (b, i, h)),
        out_shape=jax.ShapeDtypeStruct((bsz, s, MLA_HEADS * V_HEAD), BF16),
        scratch_shapes=[pltpu.VMEM((nh, 2, tq, tq), F32),
                        pltpu.VMEM((nh, 1, tq), F32),
                        pltpu.VMEM((nh, V_HEAD + BF16_ROWS, tq), F32),
                        pltpu.VMEM((tq, tq), F32)],
        compiler_params=pltpu.CompilerParams(dimension_semantics=("arbitrary", "arbitrary", "arbitrary"),
                                             vmem_limit_bytes=VMEM_LIMIT),
        name="mla_attention",
    )(q_att, k_att, vt)


def _layernorm(y, g, b):
    mu = jnp.mean(y, axis=-1, keepdims=True)
    yc = y - mu
    var = jnp.mean(yc * yc, axis=-1, keepdims=True)
    return yc * lax.rsqrt(var + 1e-5) * g + b


def _out_kernel(x_ref, og_ref, om_ref, mod_in_ref, mod_out_ref, wo_ref, ln_ref, wg_ref, wu_ref, wd_ref, o_ref,
                act_ref, x1_ref, h_ref, ff_ref, *, alpha, ff_chunk, n_blocks):
    n = pl.program_id(0)
    nw = og_ref.shape[-1]
    d_ff = wg_ref.shape[1]

    tm = x_ref.shape[1]

    def head_rows(slot, rows, mix):
        gt_m = mod_in_ref[0, 2:3, :]
        sh_f = mod_in_ref[0, 3:4, :]
        sc_f = mod_in_ref[0, 4:5, :]
        x1 = _layernorm(alpha * x_ref[0, rows, :] + gt_m * mix, ln_ref[0:1, :], ln_ref[1:2, :])
        h = (x1 * (1.0 + sc_f) + sh_f).astype(BF16)
        x1_ref[slot, rows, :] = x1
        h_ref[slot, rows, :] = h
        return h

    def mixer_proj():
        return _dot(og_ref[0], wo_ref[0:nw, :]) + _dot(om_ref[0], wo_ref[nw:, :])

    def head(slot):
        head_rows(slot, slice(0, tm), mixer_proj())

    def tail_rows(slot, rows):
        gt_f = mod_out_ref[0, 5:6, :]
        o = _layernorm(alpha * x1_ref[slot, rows, :] + gt_f * ff_ref[slot, rows, :], ln_ref[2:3, :], ln_ref[3:4, :])
        o_ref[0, rows, :] = o
        return o

    def tail(slot):
        tail_rows(slot, slice(0, tm))

    def ffn(slot, gates=None):
        for k, lo in enumerate(range(0, d_ff, ff_chunk)):
            h = h_ref[slot]
            if gates is not None and gates[k] is not None:
                h = jnp.concatenate([h[:BF16_ROWS] + gates[k], h[BF16_ROWS:]], axis=0)
            g = _dot(h, wg_ref[:, lo:lo + ff_chunk])
            u = _dot(h, wu_ref[:, lo:lo + ff_chunk])
            act_ref[:, lo:lo + ff_chunk] = (_silu(g) * u).astype(BF16)
        ff_ref[slot] = _dot(act_ref[...], wd_ref[...])

    def zero_after(*values):
        flag = None
        for v in values:
            v = v.astype(F32)
            top = v[0:SUBLANES]
            for r in range(SUBLANES, v.shape[0], SUBLANES):
                top = jnp.maximum(top, v[r:r + SUBLANES])
            over = top > HUGE
            flag = over if flag is None else (flag | over)
        z = jnp.where(flag, 1.0, 0.0)
        return jnp.concatenate([z, z], axis=0).astype(BF16)

    @pl.when(n == 0)
    def _():
        x1_ref[1] = jnp.zeros(x1_ref.shape[1:], F32)
        ff_ref[1] = jnp.zeros(ff_ref.shape[1:], F32)
        head(0)

    def steady(p):
        n_chunks = -(-d_ff // ff_chunk)
        mix = mixer_proj()
        gates = [None] * n_chunks
        piece = tm // FFN_PIECES
        for k in range(FFN_PIECES):
            rows = slice(k * piece, (k + 1) * piece)
            o = tail_rows(p, rows)
            h = head_rows(p, rows, mix[k * piece:(k + 1) * piece])
            gates[k + n_chunks - FFN_PIECES - 1] = zero_after(o, h)
        ffn(1 - p, gates)

    @pl.when((n >= 1) & (n <= n_blocks) & (n % 2 == 0))
    def _():
        steady(0)

    @pl.when((n >= 1) & (n <= n_blocks) & (n % 2 == 1))
    def _():
        steady(1)

    @pl.when(n == n_blocks + 1)
    def _():
        tail((n_blocks + 1) % 2)


def _out_ffn(x, og, om, mod, wo, ln, wg, wu, wd, *, tm, alpha, ff_chunk):
    bsz, s, d = x.shape
    d_ff = wg.shape[1]
    nb = s // tm
    n_blocks = bsz * nb
    blk_in = lambda n: jnp.minimum(n, n_blocks - 1)
    blk_out = lambda n: jnp.clip(n - 2, 0, n_blocks - 1)
    row_in = lambda w: pl.BlockSpec((1, tm, w), lambda n: (blk_in(n) // nb, blk_in(n) % nb, 0))
    mod_spec = lambda blk: pl.BlockSpec((1, mod.shape[1], d), lambda n: (blk(n) // nb, 0, 0))
    return pl.pallas_call(
        functools.partial(_out_kernel, alpha=alpha, ff_chunk=ff_chunk, n_blocks=n_blocks),
        grid=(n_blocks + 2,),
        in_specs=[row_in(d), row_in(og.shape[-1]), row_in(om.shape[-1]),
                  mod_spec(blk_in), mod_spec(blk_out),
                  _const_spec(wo.shape), _const_spec(ln.shape),
                  _const_spec(wg.shape), _const_spec(wu.shape), _const_spec(wd.shape)],
        out_specs=pl.BlockSpec((1, tm, d), lambda n: (blk_out(n) // nb, blk_out(n) % nb, 0)),
        out_shape=jax.ShapeDtypeStruct((bsz, s, d), F32),
        scratch_shapes=[pltpu.VMEM((tm, d_ff), BF16),
                        pltpu.VMEM((2, tm, d), F32),
                        pltpu.VMEM((2, tm, d), BF16),
                        pltpu.VMEM((2, tm, d), F32)],
        compiler_params=pltpu.CompilerParams(dimension_semantics=("arbitrary",),
                                             vmem_limit_bytes=VMEM_LIMIT),
        name="out_ffn",
    )(x, og, om, mod, mod, wo, ln, wg, wu, wd)


def _pack_w_in_tail(w_in):
    d = w_in.shape[0]
    split_z = DN_CONV_CH
    split_beta = split_z + DN_VW
    split_a = split_beta + DN_HEADS
    split_cq = split_a + DN_HEADS
    split_ckv = split_cq + Q_LORA
    split_kr = split_ckv + KV_LORA
    kr = w_in[:, split_kr:split_kr + QK_ROPE]
    zeros = lambda n: jnp.zeros((d, n), w_in.dtype)
    misc = jnp.concatenate([
        kr[:, :HALF_ROPE], zeros(KR2_LANE - HALF_ROPE),
        kr[:, HALF_ROPE:], zeros(BETA_LANE - KR2_LANE - HALF_ROPE),
        w_in[:, split_beta:split_a], w_in[:, split_a:split_cq],
        zeros(LANES - A_LANE - DN_HEADS)], axis=1)
    return jnp.concatenate([w_in[:, split_cq:split_kr], misc], axis=1).astype(BF16)


def _pack_w_uq(w_uq):
    q_lora = w_uq.shape[0]
    w = w_uq.reshape(q_lora, MLA_HEADS, QK_NOPE + QK_ROPE)
    zeros = lambda n: jnp.zeros((q_lora, MLA_HEADS, n), w_uq.dtype)
    tile = jnp.concatenate([
        w[:, :, :QK_NOPE],
        w[:, :, QK_NOPE:QK_NOPE + HALF_ROPE], zeros(KR2_LANE - HALF_ROPE),
        w[:, :, QK_NOPE + HALF_ROPE:], zeros(LANES - KR2_LANE - HALF_ROPE)], axis=2)
    return tile.reshape(q_lora, MLA_HEADS * QATT_W).astype(BF16)


def _pack_w_ukv(w_ukv):
    kv_lora = w_ukv.shape[0]
    w = w_ukv.reshape(kv_lora, MLA_HEADS, QK_NOPE + V_HEAD)
    return jnp.concatenate([w[:, :, :QK_NOPE].reshape(kv_lora, -1),
                            w[:, :, QK_NOPE:].reshape(kv_lora, -1)], axis=1).astype(BF16)


def _lane_table(a_log, dt_bias):
    z = jnp.zeros((LANES,), F32)
    put = lambda vec, lane: z.at[lane:lane + vec.shape[0]].set(vec)
    rows = [put(dt_bias.astype(F32), A_LANE), put(a_log.astype(F32), A_LANE)]
    rows += [z] * (SUBLANES - len(rows))
    return jnp.stack(rows)


def _inv_freq_column():
    inv_freq = 1.0 / (ROPE_THETA ** (jnp.arange(0, QK_ROPE, 2, dtype=F32) / QK_ROPE))
    return inv_freq.reshape(HALF_ROPE, 1)


def _layer(x, c_mod, pos, w_in, conv_w, a_log, dt_bias, dn_norm_g, q_norm_g, w_uq, kv_norm_g, w_ukv,
           w_o, ln1_g, ln1_b, w_gate, w_up, w_down, ln2_g, ln2_b, *, depth, tm, tg, gdn_group, tq, ff_chunk):
    alpha = (2.0 * depth) ** 0.25
    q_dn, k_dn, v_dn, zs, gb, q_att, k_att, vt = _inproj(
        x, c_mod, pos, _inv_freq_column(), w_in.astype(BF16), _pack_w_in_tail(w_in),
        conv_w.reshape(CONV_K, DN_CONV_CH).astype(F32),
        _lane_table(a_log, dt_bias), q_norm_g.reshape(1, -1).astype(F32), kv_norm_g.reshape(1, -1).astype(F32),
        _pack_w_uq(w_uq), _pack_w_ukv(w_ukv), tm=tq)
    og = _gdn(q_dn, k_dn, v_dn, gb, zs, dn_norm_g.reshape(1, -1).astype(F32), tg=tg, group=gdn_group)
    om = _attention(q_att, k_att, vt, tq=tq, nh=ATTN_HEADS_PER_STEP)
    ln = jnp.stack([ln1_g, ln1_b, ln2_g, ln2_b]).astype(F32)
    return _out_ffn(x, og, om, c_mod, w_o.astype(BF16), ln, w_gate.astype(BF16), w_up.astype(BF16),
                    w_down.astype(BF16), tm=tm, alpha=alpha, ff_chunk=ff_chunk)


def kernel(x, c, positions, w_ada, b_ada, w_in, conv_w, a_log, dt_bias, dn_norm_g, q_norm_g, w_uq, kv_norm_g, w_ukv, w_o, ln1_g, ln1_b, w_gate, w_up, w_down, ln2_g, ln2_b):
    bsz, s, d = x.shape
    depth = w_in.shape[0]
    tile = min(512, s)
    pos = positions.astype(F32).reshape(bsz, 1, s)
    for l in range(depth):
        mod = _modulation(c, w_ada[l], b_ada[l]).reshape(bsz, 6, d)
        x = _layer(x, mod, pos, w_in[l], conv_w[l], a_log[l], dt_bias[l], dn_norm_g[l], q_norm_g[l], w_uq[l],
                   kv_norm_g[l], w_ukv[l], w_o[l], ln1_g[l], ln1_b[l], w_gate[l], w_up[l], w_down[l],
                   ln2_g[l], ln2_b[l], depth=depth, tm=tile, tg=min(GDN_TILE, s), gdn_group=GDN_GROUP, tq=tile, ff_chunk=256)
    return x
```

```python
import functools
import math

import jax
import jax.numpy as jnp
from jax import lax
from jax.experimental import pallas as pl
from jax.experimental.pallas import tpu as pltpu

F32 = jnp.float32
BF16 = jnp.bfloat16

DN_HEADS = 4
DN_DK = 128
DN_DV = 128
CONV_K = 4
MLA_HEADS = 4
QK_NOPE = 128
QK_ROPE = 64
V_HEAD = 128
Q_LORA = 512
KV_LORA = 256
ROPE_THETA = 10000.0

DN_QK = DN_HEADS * DN_DK
DN_VW = DN_HEADS * DN_DV
DN_CONV_CH = 2 * DN_QK + DN_VW

LANES = 128
SUBLANES = 8
MXU_COLS = 256
GDN_CHUNK = 64
GDN_BASE = 8
LHS_ROWS = DN_DK + GDN_CHUNK
GDN_TILE = 1024
GDN_GROUP = 8
VMEM_LIMIT = 56 * 1024 * 1024

HALF_ROPE = QK_ROPE // 2
KR1_LANE = 0
KR2_LANE = 64
BETA_LANE = 96
A_LANE = 100
COL_QKV = 0
COL_Z = DN_CONV_CH
N_IN_MAIN = COL_Z + DN_VW
COL_CQ = 0
COL_CKV = COL_CQ + Q_LORA
COL_MISC = COL_CKV + KV_LORA
N_IN_TAIL = COL_MISC + LANES
QATT_W = 2 * LANES


def _dot(a, b):
    return jnp.dot(a, b, preferred_element_type=F32)


def _dot_f32(a, b):
    return jnp.dot(a, b, preferred_element_type=F32, precision=lax.Precision.HIGHEST)


def _dot_nt(a, b):
    return lax.dot_general(a, b, (((1,), (1,)), ((), ())), preferred_element_type=F32)


def _sigmoid(x):
    return 0.5 + 0.5 * jnp.tanh(0.5 * x)


def _silu(x):
    half = 0.5 * x
    return half + half * jnp.tanh(half)


def _aligned(start, multiple):
    return start if isinstance(start, int) else pl.multiple_of(start, multiple)


def _const_spec(shape):
    zeros = (0,) * len(shape)
    return pl.BlockSpec(shape, lambda *_: zeros, pipeline_mode=pl.Buffered(1))


def _mod_kernel(c_ref, w_ref, b_ref, o_ref):
    o_ref[...] = _dot(_silu(c_ref[...]), w_ref[...]) + b_ref[...]


def _modulation(c, w_ada, b_ada):
    bsz, d = c.shape
    n = w_ada.shape[1]
    return pl.pallas_call(
        _mod_kernel,
        grid=(n // d,),
        in_specs=[pl.BlockSpec((bsz, d), lambda j: (0, 0)),
                  pl.BlockSpec((d, d), lambda j: (0, j)),
                  pl.BlockSpec((1, d), lambda j: (0, j))],
        out_specs=pl.BlockSpec((bsz, d), lambda j: (0, j)),
        out_shape=jax.ShapeDtypeStruct((bsz, n), F32),
        compiler_params=pltpu.CompilerParams(dimension_semantics=("arbitrary",),
                                             vmem_limit_bytes=VMEM_LIMIT),
        name="modulation",
    )(c, w_ada, b_ada.reshape(1, n))


def _inproj_kernel(x_ref, mod_ref, pos_ref, freq_ref, win_ref, wtail_ref, convw_ref, lane_ref, qng_ref, kvng_ref,
                   wuq_ref, wukv_ref,
                   qdn_ref, kdn_ref, vdn_ref, zs_ref, gb_ref, qatt_ref, katt_ref, vt_ref,
                   halo_ref, *, tm):
    i = pl.program_id(1)
    sh_m = mod_ref[0, 0:1, :]
    sc_m = mod_ref[0, 1:2, :]
    dt_bias = lane_ref[0:1, :]
    neg_a = -jnp.exp(lane_ref[1:2, :])
    scale =math.log2(math.e) / math.sqrt(QK_NOPE + QK_ROPE)

    @pl.when(i == 0)
    def _():
        halo_ref[0:SUBLANES, :] = jnp.zeros((SUBLANES, DN_CONV_CH), F32)

    def rows_pass(r0, nr):
        rows = slice(r0, r0 + nr)
        hrows = slice(SUBLANES + r0, SUBLANES + r0 + nr)
        h = (x_ref[0, rows, :] * (1.0 + sc_m) + sh_m).astype(BF16)

        for lo in range(0, DN_CONV_CH, MXU_COLS):
            cols = slice(lo, lo + MXU_COLS)
            halo_ref[hrows, cols] = _dot(h, win_ref[:, COL_QKV + lo:COL_QKV + lo + MXU_COLS])
            conv = halo_ref[hrows, cols] * convw_ref[CONV_K - 1:CONV_K, cols]
            for d in range(1, CONV_K):
                shifted = halo_ref[SUBLANES + r0 - d:SUBLANES + r0 - d + nr, cols]
                conv = conv + shifted * convw_ref[CONV_K - 1 - d:CONV_K - d, cols]
            act = _silu(conv)
            if lo >= 2 * DN_QK:
                vdn_ref[0, rows, lo - 2 * DN_QK:lo - 2 * DN_QK + MXU_COLS] = act.astype(BF16)
                continue
            dst, off, gain = (qdn_ref, lo, DN_DK ** -0.5) if lo < DN_QK else (kdn_ref, lo - DN_QK, 1.0)
            for sub in range(0, MXU_COLS, DN_DK):
                t = act[:, sub:sub + DN_DK]
                t = t * (lax.rsqrt(jnp.sum(t * t, axis=-1, keepdims=True) + 1e-6) * gain)
                dst[0, rows, off + sub:off + sub + DN_DK] = t.astype(BF16)

        for lo in range(0, DN_VW, MXU_COLS):
            zs_ref[0, rows, lo:lo + MXU_COLS] = _silu(
                _dot(h, win_ref[:, COL_Z + lo:COL_Z + lo + MXU_COLS])).astype(BF16)

        ang = freq_ref[...] * pos_ref[0, :, rows]
        cos_f = jnp.cos(ang)
        sin_f = jnp.sin(ang)
        gap1 = jnp.zeros((KR2_LANE - HALF_ROPE, nr), F32)
        gap2 = jnp.zeros((LANES - KR2_LANE - HALF_ROPE, nr), F32)
        cos_t = jnp.concatenate([cos_f, gap1, cos_f, gap2], axis=0).T
        sin_t = jnp.concatenate([-sin_f, gap1, sin_f, gap2], axis=0).T

        def rope(t):
            return t * cos_t + pltpu.roll(t, KR2_LANE - KR1_LANE, 1) * sin_t

        misc = _dot(h, wtail_ref[:, COL_MISC:COL_MISC + LANES])
        k_rope = rope(misc).astype(BF16)
        lane = lax.broadcasted_iota(jnp.int32, misc.shape, 1)
        a_in = misc + dt_bias
        softplus = jnp.maximum(a_in, 0.0) + jnp.log(1.0 + jnp.exp(-jnp.abs(a_in)))
        gb_ref[0, rows, :] = jnp.where(
            (lane >= BETA_LANE) & (lane < BETA_LANE + DN_HEADS), _sigmoid(misc),
            jnp.where((lane >= A_LANE) & (lane < A_LANE + DN_HEADS), neg_a * softplus, 0.0))

        cq = _dot(h, wtail_ref[:, COL_CQ:COL_CQ + Q_LORA])
        cq = cq * lax.rsqrt(jnp.mean(cq * cq, axis=-1, keepdims=True) + 1e-6) * (qng_ref[...] * scale)
        qm = _dot(cq.astype(BF16), wuq_ref[...])
        for hd in range(MLA_HEADS):
            lo = hd * QATT_W
            qatt_ref[0, rows, lo:lo + LANES] = qm[:, lo:lo + LANES].astype(BF16)
            qatt_ref[0, rows, lo + LANES:lo + QATT_W] = rope(qm[:, lo + LANES:lo + QATT_W]).astype(BF16)

        ckv = _dot(h, wtail_ref[:, COL_CKV:COL_CKV + KV_LORA])
        ckv = ckv * lax.rsqrt(jnp.mean(ckv * ckv, axis=-1, keepdims=True) + 1e-6) * kvng_ref[...]
        kv = _dot(ckv.astype(BF16), wukv_ref[...])
        for hd in range(MLA_HEADS):
            lo = hd * QATT_W
            katt_ref[0, rows, lo:lo + LANES] = kv[:, hd * QK_NOPE:(hd + 1) * QK_NOPE].astype(BF16)
            katt_ref[0, rows, lo + LANES:lo + QATT_W] = k_rope
            v = kv[:, MLA_HEADS * QK_NOPE + hd * V_HEAD:MLA_HEADS * QK_NOPE + (hd + 1) * V_HEAD]
            vt_ref[0, hd, 0, :, rows] = v.T.astype(BF16)

    rows_pass(0, tm)
    halo_ref[0:SUBLANES, :] = halo_ref[tm:tm + SUBLANES, :]


def _inproj(x, mod, pos, freq, win_b, wtail, convw, lane_tab, qng, kvng, wuq_p, wukv_p, *, tm):
    bsz, s, d = x.shape
    nblk = s // tm
    row = lambda w: pl.BlockSpec((1, tm, w), lambda b, i: (b, i, 0))
    out_shape = (
        jax.ShapeDtypeStruct((bsz, s, DN_QK), BF16),
        jax.ShapeDtypeStruct((bsz, s, DN_QK), BF16),
        jax.ShapeDtypeStruct((bsz, s, DN_VW), BF16),
        jax.ShapeDtypeStruct((bsz, s, DN_VW), BF16),
        jax.ShapeDtypeStruct((bsz, s, LANES), F32),
        jax.ShapeDtypeStruct((bsz, s, MLA_HEADS * QATT_W), BF16),
        jax.ShapeDtypeStruct((bsz, s, MLA_HEADS * QATT_W), BF16),
        jax.ShapeDtypeStruct((bsz, MLA_HEADS, nblk, V_HEAD, tm), BF16),
    )
    out_specs = (row(DN_QK), row(DN_QK), row(DN_VW), row(DN_VW), row(LANES),
                 row(MLA_HEADS * QATT_W), row(MLA_HEADS * QATT_W),
                 pl.BlockSpec((1, MLA_HEADS, 1, V_HEAD, tm), lambda b, i: (b, 0, i, 0, 0)))
    return pl.pallas_call(
        functools.partial(_inproj_kernel, tm=tm),
        grid=(bsz, nblk),
        in_specs=[row(d),
                  pl.BlockSpec((1, mod.shape[1], d), lambda b, i: (b, 0, 0)),
                  pl.BlockSpec((1, 1, tm), lambda b, i: (b, 0, i)),
                  _const_spec(freq.shape), _const_spec((d, N_IN_MAIN)), _const_spec(wtail.shape),
                  _const_spec(convw.shape), _const_spec(lane_tab.shape),
                  _const_spec(qng.shape), _const_spec(kvng.shape),
                  _const_spec(wuq_p.shape), _const_spec(wukv_p.shape)],
        out_specs=out_specs,
        out_shape=out_shape,
        scratch_shapes=[pltpu.VMEM((tm + 2 * SUBLANES, DN_CONV_CH), F32)],
        compiler_params=pltpu.CompilerParams(dimension_semantics=("arbitrary", "arbitrary"),
                                             vmem_limit_bytes=VMEM_LIMIT),
        name="inproj",
    )(x, mod, pos, freq, win_b, wtail, convw, lane_tab, qng, kvng, wuq_p, wukv_p)


def _gdn_kernel(q_ref, k_ref, v_ref, gb_ref, zs_ref, gn_ref, o_ref,
                state_ref, lhs_ref, sadd_ref, oadd_ref, egl_ref, *, n_chunks, group):
    c = GDN_CHUNK

    @pl.when(pl.program_id(1) == 0)
    def _():
        state_ref[...] = jnp.zeros(state_ref.shape, F32)

    row = lax.broadcasted_iota(jnp.int32, (c, c), 0)
    col = lax.broadcasted_iota(jnp.int32, (c, c), 1)
    incl = row >= col
    strict = row > col
    tri = incl.astype(BF16)
    eye = (row == col).astype(F32)
    gnorm = gn_ref[...]
    base_shift = int(math.log2(GDN_BASE))
    diag_mask = strict & ((row >> base_shift) == (col >> base_shift))
    merge_masks = []
    for sh in range(base_shift, int(math.log2(c))):
        merge_masks.append(((row >> sh) == (col >> sh) + 1) & (((row >> sh) & 1) == 1))
    heads = range(DN_HEADS)
    lanes = [slice(hd * DN_DK, (hd + 1) * DN_DK) for hd in heads]

    def prepare(gi, slot):
        sys = [(ch, hd) for ch in range(group) for hd in heads]
        r0s = [_aligned((gi * group + ch) * c, c) for ch in range(group)]
        gbts = [gb_ref[0, pl.ds(r0, c), :] for r0 in r0s]
        gcs, gcts = [], []
        for gbt in gbts:
            hi = gbt.astype(BF16)
            r1 = gbt - hi.astype(F32)
            mid = r1.astype(BF16)
            lo = (r1 - mid.astype(F32)).astype(BF16)
            gc_all = _dot(tri, hi) + _dot(tri, mid) + _dot(tri, lo)
            gcs.append(gc_all)
            gcts.append(gc_all.T)
        gcol = [gcs[ch][:, A_LANE + hd:A_LANE + hd + 1] for ch, hd in sys]
        grow = [gcts[ch][A_LANE + hd:A_LANE + hd + 1, :] for ch, hd in sys]
        beta = [gbts[ch][:, BETA_LANE + hd:BETA_LANE + hd + 1] for ch, hd in sys]
        glast = [gcs[ch][c - 1:c, A_LANE + hd:A_LANE + hd + 1] for ch, hd in sys]
        decay = [jnp.where(incl, jnp.exp(jnp.where(incl, a - b, 0.0)), 0.0) for a, b in zip(gcol, grow)]
        kb = [k_ref[0, pl.ds(r0s[ch], c), lanes[hd]] for ch, hd in sys]
        qb = [q_ref[0, pl.ds(r0s[ch], c), lanes[hd]] for ch, hd in sys]
        kq = [_dot_nt(jnp.concatenate([k, q], axis=0), k) for k, q in zip(kb, qb)]
        a_mat = [x[:c] * b * d for x, b, d in zip(kq, beta, decay)]
        attn = [(x[c:] * d).astype(BF16) for x, d in zip(kq, decay)]
        dmat = [jnp.where(diag_mask, a, 0.0) for a in a_mat]
        t = [eye - d for d in dmat]
        p = [(-d).astype(BF16) for d in dmat]
        p = [_dot(x, x) for x in p]
        for _ in range(base_shift - 2):
            pb = [x.astype(BF16) for x in p]
            tp = [_dot(jnp.concatenate([x.astype(BF16), y], axis=0), y) for x, y in zip(t, pb)]
            t = [x + y[:c] for x, y in zip(t, tp)]
            p = [y[c:] for y in tp]
        t = [x + _dot(x.astype(BF16), y.astype(BF16)) for x, y in zip(t, p)]
        for mask in merge_masks:
            tb = [x.astype(BF16) for x in t]
            lt = [_dot(jnp.where(mask, a, 0.0).astype(BF16), x) for a, x in zip(a_mat, tb)]
            t = [x - _dot(xb, y.astype(BF16)) for x, xb, y in zip(t, tb, lt)]
        egc = [jnp.exp(x) for x in gcol]
        kf = [x.astype(F32) for x in kb]
        vf = [v_ref[0, pl.ds(r0s[ch], c), lanes[hd]].astype(F32) for ch, hd in sys]
        rhs = [jnp.concatenate([(k * (b * e)).astype(BF16), (v * b).astype(BF16)], axis=1)
               for k, v, b, e in zip(kf, vf, beta, egc)]
        wu = [_dot(x.astype(BF16), r).astype(BF16) for x, r in zip(t, rhs)]
        qg = [q.astype(F32) * e for q, e in zip(qb, egc)]
        kdt = [(k * jnp.exp(gl - gc)).T.astype(BF16) for k, gl, gc in zip(kf, glast, gcol)]
        ku = [_dot(x, y) for x, y in zip(kdt, wu)]
        au = [_dot(x, y) for x, y in zip(attn, wu)]
        for i, (ch, hd) in enumerate(sys):
            sc = slot * group + ch
            lhs_ref[hd, sc * LHS_ROWS:sc * LHS_ROWS + DN_DK, :] = (-ku[i][:, :DN_DK]).astype(BF16)
            lhs_ref[hd, sc * LHS_ROWS + DN_DK:(sc + 1) * LHS_ROWS, :] = (qg[i] - au[i][:, :DN_DK]).astype(BF16)
            sadd_ref[hd, sc * DN_DK:(sc + 1) * DN_DK, :] = ku[i][:, DN_DK:]
            oadd_ref[hd, sc * c:(sc + 1) * c, :] = au[i][:, DN_DK:]
            egl_ref[sc * DN_HEADS + hd:sc * DN_HEADS + hd + 1, :] = jnp.broadcast_to(jnp.exp(glast[i]), (1, LANES))

    def recur(gi, slot):
        for ch in range(group):
            sc = slot * group + ch
            io_rows = pl.ds(_aligned((gi * group + ch) * c, c), c)
            state = [state_ref[hd] for hd in heads]
            xs = [_dot(lhs_ref[hd, sc * LHS_ROWS:(sc + 1) * LHS_ROWS, :], state[hd].astype(BF16)) for hd in heads]
            for hd in heads:
                egl = egl_ref[sc * DN_HEADS + hd:sc * DN_HEADS + hd + 1, :]
                state_ref[hd] = state[hd] * egl + xs[hd][:DN_DK] + sadd_ref[hd, sc * DN_DK:(sc + 1) * DN_DK, :]
            for hd in heads:
                o = xs[hd][DN_DK:] + oadd_ref[hd, sc * c:(sc + 1) * c, :]
                on = o * lax.rsqrt(jnp.mean(o * o, axis=-1, keepdims=True) + 1e-6) * gnorm
                o_ref[0, io_rows, lanes[hd]] = (on * zs_ref[0, io_rows, lanes[hd]].astype(F32)).astype(BF16)

    n_groups = n_chunks // group
    prepare(0, 0)

    def pair(t, carry):
        prepare(2 * t + 1, 1)
        recur(2 * t, 0)
        prepare(2 * t + 2, 0)
        recur(2 * t + 1, 1)
        return carry

    lax.fori_loop(0, n_groups // 2 - 1, pair, 0)
    prepare(n_groups - 1, 1)
    recur(n_groups - 2, 0)
    recur(n_groups - 1, 1)


def _gdn(q_dn, k_dn, v_dn, gb, zs, gnorm, *, tg, group):
    bsz, s, _ = q_dn.shape
    n_chunks = tg // GDN_CHUNK
    assert n_chunks % (2 * group) == 0
    slots = 2 * group
    row = lambda w: pl.BlockSpec((1, tg, w), lambda b, i: (b, i, 0))
    return pl.pallas_call(
        functools.partial(_gdn_kernel, n_chunks=n_chunks, group=group),
        grid=(bsz, s // tg),
        in_specs=[row(DN_QK), row(DN_QK), row(DN_VW), row(LANES), row(DN_VW), _const_spec(gnorm.shape)],
        out_specs=row(DN_VW),
        out_shape=jax.ShapeDtypeStruct((bsz, s, DN_VW), BF16),
        scratch_shapes=[pltpu.VMEM((DN_HEADS, DN_DK, DN_DV), F32),
                        pltpu.VMEM((DN_HEADS, slots * LHS_ROWS, DN_DK), BF16),
                        pltpu.VMEM((DN_HEADS, slots * DN_DK, DN_DV), F32),
                        pltpu.VMEM((DN_HEADS, slots * GDN_CHUNK, DN_DV), F32),
                        pltpu.VMEM((slots * DN_HEADS, LANES), F32)],
        compiler_params=pltpu.CompilerParams(dimension_semantics=("arbitrary", "arbitrary"),
                                             vmem_limit_bytes=VMEM_LIMIT),
        name="gdn",
    )(q_dn, k_dn, v_dn, gb, zs, gnorm)


NEG_BIG = -1e30
ATTN_HEADS_PER_STEP = 4
FFN_PIECES = 8
BF16_ROWS = 16
HUGE = 3.0e38


def _attn_kernel(q_ref, k_ref, vt_ref, o_ref, s_ref, m_ref, acc_ref, bias_ref, *, tq, nh):
    qi = pl.program_id(2)
    heads = range(nh)

    @pl.when((pl.program_id(0) == 0) & (pl.program_id(1) == 0) & (qi == 0))
    def _():
        kidx = lax.broadcasted_iota(jnp.int32, (tq, tq), 0)
        qidx = lax.broadcasted_iota(jnp.int32, (tq, tq), 1)
        bias_ref[...] = jnp.where(kidx <= qidx, 0.0, NEG_BIG)

    q = [q_ref[0, :, h * QATT_W:(h + 1) * QATT_W] for h in heads]
    m_ref[...] = jnp.full(m_ref.shape, NEG_BIG, F32)
    acc_ref[...] = jnp.zeros(acc_ref.shape, F32)
    ones = jnp.ones((BF16_ROWS, tq), BF16)

    def scores(j, h):
        rows = pl.ds(pl.multiple_of(j * tq, tq), tq)
        return _dot_nt(k_ref[0, rows, h * QATT_W:(h + 1) * QATT_W], q[h])

    def update(j, s):
        m_old = [m_ref[h] for h in heads]
        m_new = [jnp.maximum(m_old[h], jnp.max(s[h], axis=0, keepdims=True)) for h in heads]
        alpha = [jnp.exp2(m_old[h] - m_new[h]) for h in heads]
        p = [jnp.exp2(s[h] - m_new[h]) for h in heads]
        for h in heads:
            v1 = jnp.concatenate([vt_ref[0, h, j], ones], axis=0)
            acc_ref[h] = acc_ref[h] * alpha[h] + _dot(v1, p[h].astype(BF16))
            m_ref[h] = m_new[h]

    for h in heads:
        s_ref[h, 0] = scores(0, h)

    def step(j, slot):
        for h in heads:
            s_ref[h, 1 - slot] = scores(j + 1, h)
        update(j, [s_ref[h, slot] for h in heads])

    def body(t, carry):
        step(2 * t, 0)
        step(2 * t + 1, 1)
        return carry

    lax.fori_loop(0, qi // 2, body, 0)

    @pl.when(qi % 2 == 1)
    def _():
        step(qi - 1, 0)

    bias = bias_ref[...]
    update(qi, [s_ref[h, qi & 1] + bias for h in heads])
    for h in heads:
        acc = acc_ref[h]
        o_ref[0, :, h * V_HEAD:(h + 1) * V_HEAD] = (acc[:V_HEAD] / acc[V_HEAD:V_HEAD + 1]).T.astype(BF16)


def _attention(q_att, k_att, vt, *, tq, nh):
    bsz, s, _ = q_att.shape
    nblk = s // tq
    return pl.pallas_call(
        functools.partial(_attn_kernel, tq=tq, nh=nh),
        grid=(bsz, MLA_HEADS // nh, nblk),
        in_specs=[pl.BlockSpec((1, tq, nh * QATT_W), lambda b, h, i: (b, i, h)),
                  pl.BlockSpec((1, s, nh * QATT_W), lambda b, h, i: (b, 0, h)),
                  pl.BlockSpec((1, nh, nblk, V_HEAD, tq), lambda b, h, i: (b, h, 0, 0, 0))],
        out_specs=pl.BlockSpec((1, tq, nh * V_HEAD), lambda b, h, i: (b, i, h)),
        out_shape=jax.ShapeDtypeStruct((bsz, s, MLA_HEADS * V_HEAD), BF16),
        scratch_shapes=[pltpu.VMEM((nh, 2, tq, tq), F32),
                        pltpu.VMEM((nh, 1, tq), F32),
                        pltpu.VMEM((nh, V_HEAD + BF16_ROWS, tq), F32),
                        pltpu.VMEM((tq, tq), F32)],
        compiler_params=pltpu.CompilerParams(dimension_semantics=("arbitrary", "arbitrary", "arbitrary"),
                                             vmem_limit_bytes=VMEM_LIMIT),
        name="mla_attention",
    )(q_att, k_att, vt)


def _layernorm(y, g, b):
    mu = jnp.mean(y, axis=-1, keepdims=True)
    yc = y - mu
    var = jnp.mean(yc * yc, axis=-1, keepdims=True)
    return yc * lax.rsqrt(var + 1e-5) * g + b


def _out_kernel(x_ref, og_ref, om_ref, mod_in_ref, mod_out_ref, wo_ref, ln_ref, wg_ref, wu_ref, wd_ref, o_ref,
                act_ref, x1_ref, h_ref, ff_ref, *, alpha, ff_chunk, n_blocks):
    n = pl.program_id(0)
    nw = og_ref.shape[-1]
    d_ff = wg_ref.shape[1]

    tm = x_ref.shape[1]

    def head_rows(slot, rows, mix):
        gt_m = mod_in_ref[0, 2:3, :]
        sh_f = mod_in_ref[0, 3:4, :]
        sc_f = mod_in_ref[0, 4:5, :]
        x1 = _layernorm(alpha * x_ref[0, rows, :] + gt_m * mix, ln_ref[0:1, :], ln_ref[1:2, :])
        h = (x1 * (1.0 + sc_f) + sh_f).astype(BF16)
        x1_ref[slot, rows, :] = x1
        h_ref[slot, rows, :] = h
        return h

    def mixer_proj():
        return _dot(og_ref[0], wo_ref[0:nw, :]) + _dot(om_ref[0], wo_ref[nw:, :])

    def head(slot):
        head_rows(slot, slice(0, tm), mixer_proj())

    def tail_rows(slot, rows):
        gt_f = mod_out_ref[0, 5:6, :]
        o = _layernorm(alpha * x1_ref[slot, rows, :] + gt_f * ff_ref[slot, rows, :], ln_ref[2:3, :], ln_ref[3:4, :])
        o_ref[0, rows, :] = o
        return o

    def tail(slot):
        tail_rows(slot, slice(0, tm))

    def ffn(slot, gates=None):
        for k, lo in enumerate(range(0, d_ff, ff_chunk)):
            h = h_ref[slot]
            if gates is not None and gates[k] is not None:
                h = jnp.concatenate([h[:BF16_ROWS] + gates[k], h[BF16_ROWS:]], axis=0)
            g = _dot(h, wg_ref[:, lo:lo + ff_chunk])
            u = _dot(h, wu_ref[:, lo:lo + ff_chunk])
            act_ref[:, lo:lo + ff_chunk] = (_silu(g) * u).astype(BF16)
        ff_ref[slot] = _dot(act_ref[...], wd_ref[...])

    def zero_after(*values):
        flag = None
        for v in values:
            v = v.astype(F32)
            top = v[0:SUBLANES]
            for r in range(SUBLANES, v.shape[0], SUBLANES):
                top = jnp.maximum(top, v[r:r + SUBLANES])
            over = top > HUGE
            flag = over if flag is None else (flag | over)
        z = jnp.where(flag, 1.0, 0.0)
        return jnp.concatenate([z, z], axis=0).astype(BF16)

    @pl.when(n == 0)
    def _():
        x1_ref[1] = jnp.zeros(x1_ref.shape[1:], F32)
        ff_ref[1] = jnp.zeros(ff_ref.shape[1:], F32)
        head(0)

    def steady(p):
        n_chunks = -(-d_ff // ff_chunk)
        mix = mixer_proj()
        gates = [None] * n_chunks
        piece = tm // FFN_PIECES
        for k in range(FFN_PIECES):
            rows = slice(k * piece, (k + 1) * piece)
            o = tail_rows(p, rows)
            h = head_rows(p, rows, mix[k * piece:(k + 1) * piece])
            gates[k + n_chunks - FFN_PIECES - 1] = zero_after(o, h)
        ffn(1 - p, gates)

    @pl.when((n >= 1) & (n <= n_blocks) & (n % 2 == 0))
    def _():
        steady(0)

    @pl.when((n >= 1) & (n <= n_blocks) & (n % 2 == 1))
    def _():
        steady(1)

    @pl.when(n == n_blocks + 1)
    def _():
        tail((n_blocks + 1) % 2)


def _out_ffn(x, og, om, mod, wo, ln, wg, wu, wd, *, tm, alpha, ff_chunk):
    bsz, s, d = x.shape
    d_ff = wg.shape[1]
    nb = s // tm
    n_blocks = bsz * nb
    blk_in = lambda n: jnp.minimum(n, n_blocks - 1)
    blk_out = lambda n: jnp.clip(n - 2, 0, n_blocks - 1)
    row_in = lambda w: pl.BlockSpec((1, tm, w), lambda n: (blk_in(n) // nb, blk_in(n) % nb, 0))
    mod_spec = lambda blk: pl.BlockSpec((1, mod.shape[1], d), lambda n: (blk(n) // nb, 0, 0))
    return pl.pallas_call(
        functools.partial(_out_kernel, alpha=alpha, ff_chunk=ff_chunk, n_blocks=n_blocks),
        grid=(n_blocks + 2,),
        in_specs=[row_in(d), row_in(og.shape[-1]), row_in(om.shape[-1]),
                  mod_spec(blk_in), mod_spec(blk_out),
                  _const_spec(wo.shape), _const_spec(ln.shape),
                  _const_spec(wg.shape), _const_spec(wu.shape), _const_spec(wd.shape)],
        out_specs=pl.BlockSpec((1, tm, d), lambda n: (blk_out(n) // nb, blk_out(n) % nb, 0)),
        out_shape=jax.ShapeDtypeStruct((bsz, s, d), F32),
        scratch_shapes=[pltpu.VMEM((tm, d_ff), BF16),
                        pltpu.VMEM((2, tm, d), F32),
                        pltpu.VMEM((2, tm, d), BF16),
                        pltpu.VMEM((2, tm, d), F32)],
        compiler_params=pltpu.CompilerParams(dimension_semantics=("arbitrary",),
                                             vmem_limit_bytes=VMEM_LIMIT),
        name="out_ffn",
    )(x, og, om, mod, mod, wo, ln, wg, wu, wd)


def _pack_w_in_tail(w_in):
    d = w_in.shape[0]
    split_z = DN_CONV_CH
    split_beta = split_z + DN_VW
    split_a = split_beta + DN_HEADS
    split_cq = split_a + DN_HEADS
    split_ckv = split_cq + Q_LORA
    split_kr = split_ckv + KV_LORA
    kr = w_in[:, split_kr:split_kr + QK_ROPE]
    zeros = lambda n: jnp.zeros((d, n), w_in.dtype)
    misc = jnp.concatenate([
        kr[:, :HALF_ROPE], zeros(KR2_LANE - HALF_ROPE),
        kr[:, HALF_ROPE:], zeros(BETA_LANE - KR2_LANE - HALF_ROPE),
        w_in[:, split_beta:split_a], w_in[:, split_a:split_cq],
        zeros(LANES - A_LANE - DN_HEADS)], axis=1)
    return jnp.concatenate([w_in[:, split_cq:split_kr], misc], axis=1).astype(BF16)


def _pack_w_uq(w_uq):
    q_lora = w_uq.shape[0]
    w = w_uq.reshape(q_lora, MLA_HEADS, QK_NOPE + QK_ROPE)
    zeros = lambda n: jnp.zeros((q_lora, MLA_HEADS, n), w_uq.dtype)
    tile = jnp.concatenate([
        w[:, :, :QK_NOPE],
        w[:, :, QK_NOPE:QK_NOPE + HALF_ROPE], zeros(KR2_LANE - HALF_ROPE),
        w[:, :, QK_NOPE + HALF_ROPE:], zeros(LANES - KR2_LANE - HALF_ROPE)], axis=2)
    return tile.reshape(q_lora, MLA_HEADS * QATT_W).astype(BF16)


def _pack_w_ukv(w_ukv):
    kv_lora = w_ukv.shape[0]
    w = w_ukv.reshape(kv_lora, MLA_HEADS, QK_NOPE + V_HEAD)
    return jnp.concatenate([w[:, :, :QK_NOPE].reshape(kv_lora, -1),
                            w[:, :, QK_NOPE:].reshape(kv_lora, -1)], axis=1).astype(BF16)


def _lane_table(a_log, dt_bias):
    z = jnp.zeros((LANES,), F32)
    put = lambda vec, lane: z.at[lane:lane + vec.shape[0]].set(vec)
    rows = [put(dt_bias.astype(F32), A_LANE), put(a_log.astype(F32), A_LANE)]
    rows += [z] * (SUBLANES - len(rows))
    return jnp.stack(rows)


def _inv_freq_column():
    inv_freq = 1.0 / (ROPE_THETA ** (jnp.arange(0, QK_ROPE, 2, dtype=F32) / QK_ROPE))
    return inv_freq.reshape(HALF_ROPE, 1)


def _layer(x, c_mod, pos, w_in, conv_w, a_log, dt_bias, dn_norm_g, q_norm_g, w_uq, kv_norm_g, w_ukv,
           w_o, ln1_g, ln1_b, w_gate, w_up, w_down, ln2_g, ln2_b, *, depth, tm, tg, gdn_group, tq, ff_chunk):
    alpha = (2.0 * depth) ** 0.25
    q_dn, k_dn, v_dn, zs, gb, q_att, k_att, vt = _inproj(
        x, c_mod, pos, _inv_freq_column(), w_in.astype(BF16), _pack_w_in_tail(w_in),
        conv_w.reshape(CONV_K, DN_CONV_CH).astype(F32),
        _lane_table(a_log, dt_bias), q_norm_g.reshape(1, -1).astype(F32), kv_norm_g.reshape(1, -1).astype(F32),
        _pack_w_uq(w_uq), _pack_w_ukv(w_ukv), tm=tq)
    og = _gdn(q_dn, k_dn, v_dn, gb, zs, dn_norm_g.reshape(1, -1).astype(F32), tg=tg, group=gdn_group)
    om = _attention(q_att, k_att, vt, tq=tq, nh=ATTN_HEADS_PER_STEP)
    ln = jnp.stack([ln1_g, ln1_b, ln2_g, ln2_b]).astype(F32)
    return _out_ffn(x, og, om, c_mod, w_o.astype(BF16), ln, w_gate.astype(BF16), w_up.astype(BF16),
                    w_down.astype(BF16), tm=tm, alpha=alpha, ff_chunk=ff_chunk)


def kernel(x, c, positions, w_ada, b_ada, w_in, conv_w, a_log, dt_bias, dn_norm_g, q_norm_g, w_uq, kv_norm_g, w_ukv, w_o, ln1_g, ln1_b, w_gate, w_up, w_down, ln2_g, ln2_b):
    bsz, s, d = x.shape
    depth = w_in.shape[0]
    tile = min(512, s)
    pos = positions.astype(F32).reshape(bsz, 1, s)
    for l in range(depth):
        mod = _modulation(c, w_ada[l], b_ada[l]).reshape(bsz, 6, d)
        x = _layer(x, mod, pos, w_in[l], conv_w[l], a_log[l], dt_bias[l], dn_norm_g[l], q_norm_g[l], w_uq[l],
                   kv_norm_g[l], w_ukv[l], w_o[l], ln1_g[l], ln1_b[l], w_gate[l], w_up[l], w_down[l],
                   ln2_g[l], ln2_b[l], depth=depth, tm=tile, tg=min(GDN_TILE, s), gdn_group=GDN_GROUP, tq=tile, ff_chunk=256)
    return x
```

```python
import functools
import math

import jax
import jax.numpy as jnp
from jax import lax
from jax.experimental import pallas as pl
from jax.experimental.pallas import tpu as pltpu

F32 = jnp.float32
BF16 = jnp.bfloat16

DN_HEADS = 4
DN_DK = 128
DN_DV = 128
CONV_K = 4
MLA_HEADS = 4
QK_NOPE = 128
QK_ROPE = 64
V_HEAD = 128
Q_LORA = 512
KV_LORA = 256
ROPE_THETA = 10000.0

DN_QK = DN_HEADS * DN_DK
DN_VW = DN_HEADS * DN_DV
DN_CONV_CH = 2 * DN_QK + DN_VW

LANES = 128
SUBLANES = 8
MXU_COLS = 256
GDN_CHUNK = 128
GDN_BASE = 8
LHS_ROWS = DN_DK + GDN_CHUNK
GDN_TILE = 1024
GDN_GROUP = 4
VMEM_LIMIT = 56 * 1024 * 1024

HALF_ROPE = QK_ROPE // 2
KR1_LANE = 0
KR2_LANE = 64
BETA_LANE = 96
A_LANE = 100
COL_QKV = 0
COL_Z = DN_CONV_CH
N_IN_MAIN = COL_Z + DN_VW
COL_CQ = 0
COL_CKV = COL_CQ + Q_LORA
COL_MISC = COL_CKV + KV_LORA
N_IN_TAIL = COL_MISC + LANES
QATT_W = 2 * LANES


def _dot(a, b):
    return jnp.dot(a, b, preferred_element_type=F32)


def _dot_f32(a, b):
    return jnp.dot(a, b, preferred_element_type=F32, precision=lax.Precision.HIGHEST)


def _dot_nt(a, b):
    return lax.dot_general(a, b, (((1,), (1,)), ((), ())), preferred_element_type=F32)


def _sigmoid(x):
    return 0.5 + 0.5 * jnp.tanh(0.5 * x)


def _silu(x):
    half = 0.5 * x
    return half + half * jnp.tanh(half)


def _aligned(start, multiple):
    return start if isinstance(start, int) else pl.multiple_of(start, multiple)


def _const_spec(shape):
    zeros = (0,) * len(shape)
    return pl.BlockSpec(shape, lambda *_: zeros, pipeline_mode=pl.Buffered(1))


def _mod_kernel(c_ref, w_ref, b_ref, o_ref):
    o_ref[...] = _dot(_silu(c_ref[...]), w_ref[...]) + b_ref[...]


def _modulation(c, w_ada, b_ada):
    bsz, d = c.shape
    n = w_ada.shape[1]
    return pl.pallas_call(
        _mod_kernel,
        grid=(n // d,),
        in_specs=[pl.BlockSpec((bsz, d), lambda j: (0, 0)),
                  pl.BlockSpec((d, d), lambda j: (0, j)),
                  pl.BlockSpec((1, d), lambda j: (0, j))],
        out_specs=pl.BlockSpec((bsz, d), lambda j: (0, j)),
        out_shape=jax.ShapeDtypeStruct((bsz, n), F32),
        compiler_params=pltpu.CompilerParams(dimension_semantics=("arbitrary",),
                                             vmem_limit_bytes=VMEM_LIMIT),
        name="modulation",
    )(c, w_ada, b_ada.reshape(1, n))


def _inproj_kernel(x_ref, mod_ref, pos_ref, freq_ref, win_ref, wtail_ref, convw_ref, lane_ref, qng_ref, kvng_ref,
                   wuq_ref, wukv_ref,
                   qdn_ref, kdn_ref, vdn_ref, zs_ref, gb_ref, qatt_ref, katt_ref, vt_ref,
                   halo_ref, *, tm):
    i = pl.program_id(1)
    sh_m = mod_ref[0, 0:1, :]
    sc_m = mod_ref[0, 1:2, :]
    dt_bias = lane_ref[0:1, :]
    neg_a = -jnp.exp(lane_ref[1:2, :])
    scale =math.log2(math.e) / math.sqrt(QK_NOPE + QK_ROPE)

    @pl.when(i == 0)
    def _():
        halo_ref[0:SUBLANES, :] = jnp.zeros((SUBLANES, DN_CONV_CH), F32)

    def rows_pass(r0, nr):
        rows = slice(r0, r0 + nr)
        hrows = slice(SUBLANES + r0, SUBLANES + r0 + nr)
        h = (x_ref[0, rows, :] * (1.0 + sc_m) + sh_m).astype(BF16)

        for lo in range(0, DN_CONV_CH, MXU_COLS):
            cols = slice(lo, lo + MXU_COLS)
            halo_ref[hrows, cols] = _dot(h, win_ref[:, COL_QKV + lo:COL_QKV + lo + MXU_COLS])
            conv = halo_ref[hrows, cols] * convw_ref[CONV_K - 1:CONV_K, cols]
            for d in range(1, CONV_K):
                shifted = halo_ref[SUBLANES + r0 - d:SUBLANES + r0 - d + nr, cols]
                conv = conv + shifted * convw_ref[CONV_K - 1 - d:CONV_K - d, cols]
            act = _silu(conv)
            if lo >= 2 * DN_QK:
                vdn_ref[0, rows, lo - 2 * DN_QK:lo - 2 * DN_QK + MXU_COLS] = act.astype(BF16)
                continue
            dst, off, gain = (qdn_ref, lo, DN_DK ** -0.5) if lo < DN_QK else (kdn_ref, lo - DN_QK, 1.0)
            for sub in range(0, MXU_COLS, DN_DK):
                t = act[:, sub:sub + DN_DK]
                t = t * (lax.rsqrt(jnp.sum(t * t, axis=-1, keepdims=True) + 1e-6) * gain)
                dst[0, rows, off + sub:off + sub + DN_DK] = t.astype(BF16)

        for lo in range(0, DN_VW, MXU_COLS):
            zs_ref[0, rows, lo:lo + MXU_COLS] = _silu(
                _dot(h, win_ref[:, COL_Z + lo:COL_Z + lo + MXU_COLS])).astype(BF16)

        ang = freq_ref[...] * pos_ref[0, :, rows]
        cos_f = jnp.cos(ang)
        sin_f = jnp.sin(ang)
        gap1 = jnp.zeros((KR2_LANE - HALF_ROPE, nr), F32)
        gap2 = jnp.zeros((LANES - KR2_LANE - HALF_ROPE, nr), F32)
        cos_t = jnp.concatenate([cos_f, gap1, cos_f, gap2], axis=0).T
        sin_t = jnp.concatenate([-sin_f, gap1, sin_f, gap2], axis=0).T

        def rope(t):
            return t * cos_t + pltpu.roll(t, KR2_LANE - KR1_LANE, 1) * sin_t

        misc = _dot(h, wtail_ref[:, COL_MISC:COL_MISC + LANES])
        k_rope = rope(misc).astype(BF16)
        lane = lax.broadcasted_iota(jnp.int32, misc.shape, 1)
        a_in = misc + dt_bias
        softplus = jnp.maximum(a_in, 0.0) + jnp.log(1.0 + jnp.exp(-jnp.abs(a_in)))
        gb_ref[0, rows, :] = jnp.where(
            (lane >= BETA_LANE) & (lane < BETA_LANE + DN_HEADS), _sigmoid(misc),
            jnp.where((lane >= A_LANE) & (lane < A_LANE + DN_HEADS), neg_a * softplus, 0.0))

        cq = _dot(h, wtail_ref[:, COL_CQ:COL_CQ + Q_LORA])
        cq = cq * lax.rsqrt(jnp.mean(cq * cq, axis=-1, keepdims=True) + 1e-6) * (qng_ref[...] * scale)
        qm = _dot(cq.astype(BF16), wuq_ref[...])
        for hd in range(MLA_HEADS):
            lo = hd * QATT_W
            qatt_ref[0, rows, lo:lo + LANES] = qm[:, lo:lo + LANES].astype(BF16)
            qatt_ref[0, rows, lo + LANES:lo + QATT_W] = rope(qm[:, lo + LANES:lo + QATT_W]).astype(BF16)

        ckv = _dot(h, wtail_ref[:, COL_CKV:COL_CKV + KV_LORA])
        ckv = ckv * lax.rsqrt(jnp.mean(ckv * ckv, axis=-1, keepdims=True) + 1e-6) * kvng_ref[...]
        kv = _dot(ckv.astype(BF16), wukv_ref[...])
        for hd in range(MLA_HEADS):
            lo = hd * QATT_W
            katt_ref[0, rows, lo:lo + LANES] = kv[:, hd * QK_NOPE:(hd + 1) * QK_NOPE].astype(BF16)
            katt_ref[0, rows, lo + LANES:lo + QATT_W] = k_rope
            v = kv[:, MLA_HEADS * QK_NOPE + hd * V_HEAD:MLA_HEADS * QK_NOPE + (hd + 1) * V_HEAD]
            vt_ref[0, hd, 0, :, rows] = v.T.astype(BF16)

    rows_pass(0, tm)
    halo_ref[0:SUBLANES, :] = halo_ref[tm:tm + SUBLANES, :]


def _inproj(x, mod, pos, freq, win_b, wtail, convw, lane_tab, qng, kvng, wuq_p, wukv_p, *, tm):
    bsz, s, d = x.shape
    nblk = s // tm
    row = lambda w: pl.BlockSpec((1, tm, w), lambda b, i: (b, i, 0))
    out_shape = (
        jax.ShapeDtypeStruct((bsz, s, DN_QK), BF16),
        jax.ShapeDtypeStruct((bsz, s, DN_QK), BF16),
        jax.ShapeDtypeStruct((bsz, s, DN_VW), BF16),
        jax.ShapeDtypeStruct((bsz, s, DN_VW), BF16),
        jax.ShapeDtypeStruct((bsz, s, LANES), F32),
        jax.ShapeDtypeStruct((bsz, s, MLA_HEADS * QATT_W), BF16),
        jax.ShapeDtypeStruct((bsz, s, MLA_HEADS * QATT_W), BF16),
        jax.ShapeDtypeStruct((bsz, MLA_HEADS, nblk, V_HEAD, tm), BF16),
    )
    out_specs = (row(DN_QK), row(DN_QK), row(DN_VW), row(DN_VW), row(LANES),
                 row(MLA_HEADS * QATT_W), row(MLA_HEADS * QATT_W),
                 pl.BlockSpec((1, MLA_HEADS, 1, V_HEAD, tm), lambda b, i: (b, 0, i, 0, 0)))
    return pl.pallas_call(
        functools.partial(_inproj_kernel, tm=tm),
        grid=(bsz, nblk),
        in_specs=[row(d),
                  pl.BlockSpec((1, mod.shape[1], d), lambda b, i: (b, 0, 0)),
                  pl.BlockSpec((1, 1, tm), lambda b, i: (b, 0, i)),
                  _const_spec(freq.shape), _const_spec((d, N_IN_MAIN)), _const_spec(wtail.shape),
                  _const_spec(convw.shape), _const_spec(lane_tab.shape),
                  _const_spec(qng.shape), _const_spec(kvng.shape),
                  _const_spec(wuq_p.shape), _const_spec(wukv_p.shape)],
        out_specs=out_specs,
        out_shape=out_shape,
        scratch_shapes=[pltpu.VMEM((tm + 2 * SUBLANES, DN_CONV_CH), F32)],
        compiler_params=pltpu.CompilerParams(dimension_semantics=("arbitrary", "arbitrary"),
                                             vmem_limit_bytes=VMEM_LIMIT),
        name="inproj",
    )(x, mod, pos, freq, win_b, wtail, convw, lane_tab, qng, kvng, wuq_p, wukv_p)


def _gdn_kernel(q_ref, k_ref, v_ref, gb_ref, zs_ref, gn_ref, o_ref,
                state_ref, lhs_ref, sadd_ref, oadd_ref, egl_ref, *, n_chunks, group):
    c = GDN_CHUNK

    @pl.when(pl.program_id(1) == 0)
    def _():
        state_ref[...] = jnp.zeros(state_ref.shape, F32)

    row = lax.broadcasted_iota(jnp.int32, (c, c), 0)
    col = lax.broadcasted_iota(jnp.int32, (c, c), 1)
    incl = row >= col
    strict = row > col
    tri = incl.astype(BF16)
    eye = (row == col).astype(F32)
    gnorm = gn_ref[...]
    base_shift = int(math.log2(GDN_BASE))
    diag_mask = strict & ((row >> base_shift) == (col >> base_shift))
    merge_masks = []
    for sh in range(base_shift, int(math.log2(c))):
        merge_masks.append(((row >> sh) == (col >> sh) + 1) & (((row >> sh) & 1) == 1))
    heads = range(DN_HEADS)
    lanes = [slice(hd * DN_DK, (hd + 1) * DN_DK) for hd in heads]

    def prepare(gi, slot):
        sys = [(ch, hd) for ch in range(group) for hd in heads]
        r0s = [_aligned((gi * group + ch) * c, c) for ch in range(group)]
        gbts = [gb_ref[0, pl.ds(r0, c), :] for r0 in r0s]
        gcs, gcts = [], []
        for gbt in gbts:
            hi = gbt.astype(BF16)
            r1 = gbt - hi.astype(F32)
            mid = r1.astype(BF16)
            lo = (r1 - mid.astype(F32)).astype(BF16)
            gc_all = _dot(tri, hi) + _dot(tri, mid) + _dot(tri, lo)
            gcs.append(gc_all)
            gcts.append(gc_all.T)
        gcol = [gcs[ch][:, A_LANE + hd:A_LANE + hd + 1] for ch, hd in sys]
        grow = [gcts[ch][A_LANE + hd:A_LANE + hd + 1, :] for ch, hd in sys]
        beta = [gbts[ch][:, BETA_LANE + hd:BETA_LANE + hd + 1] for ch, hd in sys]
        glast = [gcs[ch][c - 1:c, A_LANE + hd:A_LANE + hd + 1] for ch, hd in sys]
        decay = [jnp.where(incl, jnp.exp(jnp.where(incl, a - b, 0.0)), 0.0) for a, b in zip(gcol, grow)]
        kb = [k_ref[0, pl.ds(r0s[ch], c), lanes[hd]] for ch, hd in sys]
        qb = [q_ref[0, pl.ds(r0s[ch], c), lanes[hd]] for ch, hd in sys]
        kq = [_dot_nt(jnp.concatenate([k, q], axis=0), k) for k, q in zip(kb, qb)]
        a_mat = [x[:c] * b * d for x, b, d in zip(kq, beta, decay)]
        attn = [(x[c:] * d).astype(BF16) for x, d in zip(kq, decay)]
        dmat = [jnp.where(diag_mask, a, 0.0) for a in a_mat]
        t = [eye - d for d in dmat]
        p = [(-d).astype(BF16) for d in dmat]
        p = [_dot(x, x) for x in p]
        for _ in range(base_shift - 2):
            pb = [x.astype(BF16) for x in p]
            tp = [_dot(jnp.concatenate([x.astype(BF16), y], axis=0), y) for x, y in zip(t, pb)]
            t = [x + y[:c] for x, y in zip(t, tp)]
            p = [y[c:] for y in tp]
        t = [x + _dot(x.astype(BF16), y.astype(BF16)) for x, y in zip(t, p)]
        for mask in merge_masks:
            tb = [x.astype(BF16) for x in t]
            lt = [_dot(jnp.where(mask, a, 0.0).astype(BF16), x) for a, x in zip(a_mat, tb)]
            t = [x - _dot(xb, y.astype(BF16)) for x, xb, y in zip(t, tb, lt)]
        egc = [jnp.exp(x) for x in gcol]
        kf = [x.astype(F32) for x in kb]
        vf = [v_ref[0, pl.ds(r0s[ch], c), lanes[hd]].astype(F32) for ch, hd in sys]
        rhs = [jnp.concatenate([(k * (b * e)).astype(BF16), (v * b).astype(BF16)], axis=1)
               for k, v, b, e in zip(kf, vf, beta, egc)]
        wu = [_dot(x.astype(BF16), r).astype(BF16) for x, r in zip(t, rhs)]
        qg = [q.astype(F32) * e for q, e in zip(qb, egc)]
        kdt = [(k * jnp.exp(gl - gc)).T.astype(BF16) for k, gl, gc in zip(kf, glast, gcol)]
        ku = [_dot(x, y) for x, y in zip(kdt, wu)]
        au = [_dot(x, y) for x, y in zip(attn, wu)]
        for i, (ch, hd) in enumerate(sys):
            sc = slot * group + ch
            lhs_ref[hd, sc * LHS_ROWS:sc * LHS_ROWS + DN_DK, :] = (-ku[i][:, :DN_DK]).astype(BF16)
            lhs_ref[hd, sc * LHS_ROWS + DN_DK:(sc + 1) * LHS_ROWS, :] = (qg[i] - au[i][:, :DN_DK]).astype(BF16)
            sadd_ref[hd, sc * DN_DK:(sc + 1) * DN_DK, :] = ku[i][:, DN_DK:]
            oadd_ref[hd, sc * c:(sc + 1) * c, :] = au[i][:, DN_DK:]
            egl_ref[sc * DN_HEADS + hd:sc * DN_HEADS + hd + 1, :] = jnp.broadcast_to(jnp.exp(glast[i]), (1, LANES))

    def recur(gi, slot):
        for ch in range(group):
            sc = slot * group + ch
            io_rows = pl.ds(_aligned((gi * group + ch) * c, c), c)
            state = [state_ref[hd] for hd in heads]
            xs = [_dot(lhs_ref[hd, sc * LHS_ROWS:(sc + 1) * LHS_ROWS, :], state[hd].astype(BF16)) for hd in heads]
            for hd in heads:
                egl = egl_ref[sc * DN_HEADS + hd:sc * DN_HEADS + hd + 1, :]
                state_ref[hd] = state[hd] * egl + xs[hd][:DN_DK] + sadd_ref[hd, sc * DN_DK:(sc + 1) * DN_DK, :]
            for hd in heads:
                o = xs[hd][DN_DK:] + oadd_ref[hd, sc * c:(sc + 1) * c, :]
                on = o * lax.rsqrt(jnp.mean(o * o, axis=-1, keepdims=True) + 1e-6) * gnorm
                o_ref[0, io_rows, lanes[hd]] = (on * zs_ref[0, io_rows, lanes[hd]].astype(F32)).astype(BF16)

    n_groups = n_chunks // group
    prepare(0, 0)

    def pair(t, carry):
        prepare(2 * t + 1, 1)
        recur(2 * t, 0)
        prepare(2 * t + 2, 0)
        recur(2 * t + 1, 1)
        return carry

    lax.fori_loop(0, n_groups // 2 - 1, pair, 0)
    prepare(n_groups - 1, 1)
    recur(n_groups - 2, 0)
    recur(n_groups - 1, 1)


def _gdn(q_dn, k_dn, v_dn, gb, zs, gnorm, *, tg, group):
    bsz, s, _ = q_dn.shape
    n_chunks = tg // GDN_CHUNK
    assert n_chunks % (2 * group) == 0
    slots = 2 * group
    row = lambda w: pl.BlockSpec((1, tg, w), lambda b, i: (b, i, 0))
    return pl.pallas_call(
        functools.partial(_gdn_kernel, n_chunks=n_chunks, group=group),
        grid=(bsz, s // tg),
        in_specs=[row(DN_QK), row(DN_QK), row(DN_VW), row(LANES), row(DN_VW), _const_spec(gnorm.shape)],
        out_specs=row(DN_VW),
        out_shape=jax.ShapeDtypeStruct((bsz, s, DN_VW), BF16),
        scratch_shapes=[pltpu.VMEM((DN_HEADS, DN_DK, DN_DV), F32),
                        pltpu.VMEM((DN_HEADS, slots * LHS_ROWS, DN_DK), BF16),
                        pltpu.VMEM((DN_HEADS, slots * DN_DK, DN_DV), F32),
                        pltpu.VMEM((DN_HEADS, slots * GDN_CHUNK, DN_DV), F32),
                        pltpu.VMEM((slots * DN_HEADS, LANES), F32)],
        compiler_params=pltpu.CompilerParams(dimension_semantics=("arbitrary", "arbitrary"),
                                             vmem_limit_bytes=VMEM_LIMIT),
        name="gdn",
    )(q_dn, k_dn, v_dn, gb, zs, gnorm)


NEG_BIG = -1e30
ATTN_HEADS_PER_STEP = 4
FFN_PIECES = 8
BF16_ROWS = 16
HUGE = 3.0e38


def _attn_kernel(q_ref, k_ref, vt_ref, o_ref, s_ref, m_ref, acc_ref, bias_ref, *, tq, nh):
    qi = pl.program_id(2)
    heads = range(nh)

    @pl.when((pl.program_id(0) == 0) & (pl.program_id(1) == 0) & (qi == 0))
    def _():
        kidx = lax.broadcasted_iota(jnp.int32, (tq, tq), 0)
        qidx = lax.broadcasted_iota(jnp.int32, (tq, tq), 1)
        bias_ref[...] = jnp.where(kidx <= qidx, 0.0, NEG_BIG)

    q = [q_ref[0, :, h * QATT_W:(h + 1) * QATT_W] for h in heads]
    m_ref[...] = jnp.full(m_ref.shape, NEG_BIG, F32)
    acc_ref[...] = jnp.zeros(acc_ref.shape, F32)
    ones = jnp.ones((BF16_ROWS, tq), BF16)

    def scores(j, h):
        rows = pl.ds(pl.multiple_of(j * tq, tq), tq)
        return _dot_nt(k_ref[0, rows, h * QATT_W:(h + 1) * QATT_W], q[h])

    def update(j, s):
        m_old = [m_ref[h] for h in heads]
        m_new = [jnp.maximum(m_old[h], jnp.max(s[h], axis=0, keepdims=True)) for h in heads]
        alpha = [jnp.exp2(m_old[h] - m_new[h]) for h in heads]
        p = [jnp.exp2(s[h] - m_new[h]) for h in heads]
        for h in heads:
            v1 = jnp.concatenate([vt_ref[0, h, j], ones], axis=0)
            acc_ref[h] = acc_ref[h] * alpha[h] + _dot(v1, p[h].astype(BF16))
            m_ref[h] = m_new[h]

    for h in heads:
        s_ref[h, 0] = scores(0, h)

    def step(j, slot):
        for h in heads:
            s_ref[h, 1 - slot] = scores(j + 1, h)
        update(j, [s_ref[h, slot] for h in heads])

    def body(t, carry):
        step(2 * t, 0)
        step(2 * t + 1, 1)
        return carry

    lax.fori_loop(0, qi // 2, body, 0)

    @pl.when(qi % 2 == 1)
    def _():
        step(qi - 1, 0)

    bias = bias_ref[...]
    update(qi, [s_ref[h, qi & 1] + bias for h in heads])
    for h in heads:
        acc = acc_ref[h]
        o_ref[0, :, h * V_HEAD:(h + 1) * V_HEAD] = (acc[:V_HEAD] / acc[V_HEAD:V_HEAD + 1]).T.astype(BF16)


def _attention(q_att, k_att, vt, *, tq, nh):
    bsz, s, _ = q_att.shape
    nblk = s // tq
    return pl.pallas_call(
        functools.partial(_attn_kernel, tq=tq, nh=nh),
        grid=(bsz, MLA_HEADS // nh, nblk),
        in_specs=[pl.BlockSpec((1, tq, nh * QATT_W), lambda b, h, i: (b, i, h)),
                  pl.BlockSpec((1, s, nh * QATT_W), lambda b, h, i: (b, 0, h)),
                  pl.BlockSpec((1, nh, nblk, V_HEAD, tq), lambda b, h, i: (b, h, 0, 0, 0))],
        out_specs=pl.BlockSpec((1, tq, nh * V_HEAD), lambda b, h, i: (b, i, h)),
        out_shape=jax.ShapeDtypeStruct((bsz, s, MLA_HEADS * V_HEAD), BF16),
        scratch_shapes=[pltpu.VMEM((nh, 2, tq, tq), F32),
                        pltpu.VMEM((nh, 1, tq), F32),
                        pltpu.VMEM((nh, V_HEAD + BF16_ROWS, tq), F32),
                        pltpu.VMEM((tq, tq), F32)],
        compiler_params=pltpu.CompilerParams(dimension_semantics=("arbitrary", "arbitrary", "arbitrary"),
                                             vmem_limit_bytes=VMEM_LIMIT),
        name="mla_attention",
    )(q_att, k_att, vt)


def _layernorm(y, g, b):
    mu = jnp.mean(y, axis=-1, keepdims=True)
    yc = y - mu
    var = jnp.mean(yc * yc, axis=-1, keepdims=True)
    return yc * lax.rsqrt(var + 1e-5) * g + b


def _out_kernel(x_ref, og_ref, om_ref, mod_in_ref, mod_out_ref, wo_ref, ln_ref, wg_ref, wu_ref, wd_ref, o_ref,
                act_ref, x1_ref, h_ref, ff_ref, *, alpha, ff_chunk, n_blocks):
    n = pl.program_id(0)
    nw = og_ref.shape[-1]
    d_ff = wg_ref.shape[1]

    tm = x_ref.shape[1]

    def head_rows(slot, rows, mix):
        gt_m = mod_in_ref[0, 2:3, :]
        sh_f = mod_in_ref[0, 3:4, :]
        sc_f = mod_in_ref[0, 4:5, :]
        x1 = _layernorm(alpha * x_ref[0, rows, :] + gt_m * mix, ln_ref[0:1, :], ln_ref[1:2, :])
        h = (x1 * (1.0 + sc_f) + sh_f).astype(BF16)
        x1_ref[slot, rows, :] = x1
        h_ref[slot, rows, :] = h
        return h

    def mixer_proj():
        return _dot(og_ref[0], wo_ref[0:nw, :]) + _dot(om_ref[0], wo_ref[nw:, :])

    def head(slot):
        head_rows(slot, slice(0, tm), mixer_proj())

    def tail_rows(slot, rows):
        gt_f = mod_out_ref[0, 5:6, :]
        o = _layernorm(alpha * x1_ref[slot, rows, :] + gt_f * ff_ref[slot, rows, :], ln_ref[2:3, :], ln_ref[3:4, :])
        o_ref[0, rows, :] = o
        return o

    def tail(slot):
        tail_rows(slot, slice(0, tm))

    def ffn(slot, gates=None):
        for k, lo in enumerate(range(0, d_ff, ff_chunk)):
            h = h_ref[slot]
            if gates is not None and gates[k] is not None:
                h = jnp.concatenate([h[:BF16_ROWS] + gates[k], h[BF16_ROWS:]], axis=0)
            g = _dot(h, wg_ref[:, lo:lo + ff_chunk])
            u = _dot(h, wu_ref[:, lo:lo + ff_chunk])
            act_ref[:, lo:lo + ff_chunk] = (_silu(g) * u).astype(BF16)
        ff_ref[slot] = _dot(act_ref[...], wd_ref[...])

    def zero_after(*values):
        flag = None
        for v in values:
            v = v.astype(F32)
            top = v[0:SUBLANES]
            for r in range(SUBLANES, v.shape[0], SUBLANES):
                top = jnp.maximum(top, v[r:r + SUBLANES])
            over = top > HUGE
            flag = over if flag is None else (flag | over)
        z = jnp.where(flag, 1.0, 0.0)
        return jnp.concatenate([z, z], axis=0).astype(BF16)

    @pl.when(n == 0)
    def _():
        x1_ref[1] = jnp.zeros(x1_ref.shape[1:], F32)
        ff_ref[1] = jnp.zeros(ff_ref.shape[1:], F32)
        head(0)

    def steady(p):
        n_chunks = -(-d_ff // ff_chunk)
        mix = mixer_proj()
        gates = [None] * n_chunks
        piece = tm // FFN_PIECES
        for k in range(FFN_PIECES):
            rows = slice(k * piece, (k + 1) * piece)
            o = tail_rows(p, rows)
            h = head_rows(p, rows, mix[k * piece:(k + 1) * piece])
            gates[k + n_chunks - FFN_PIECES - 1] = zero_after(o, h)
        ffn(1 - p, gates)

    @pl.when((n >= 1) & (n <= n_blocks) & (n % 2 == 0))
    def _():
        steady(0)

    @pl.when((n >= 1) & (n <= n_blocks) & (n % 2 == 1))
    def _():
        steady(1)

    @pl.when(n == n_blocks + 1)
    def _():
        tail((n_blocks + 1) % 2)


def _out_ffn(x, og, om, mod, wo, ln, wg, wu, wd, *, tm, alpha, ff_chunk):
    bsz, s, d = x.shape
    d_ff = wg.shape[1]
    nb = s // tm
    n_blocks = bsz * nb
    blk_in = lambda n: jnp.minimum(n, n_blocks - 1)
    blk_out = lambda n: jnp.clip(n - 2, 0, n_blocks - 1)
    row_in = lambda w: pl.BlockSpec((1, tm, w), lambda n: (blk_in(n) // nb, blk_in(n) % nb, 0))
    mod_spec = lambda blk: pl.BlockSpec((1, mod.shape[1], d), lambda n: (blk(n) // nb, 0, 0))
    return pl.pallas_call(
        functools.partial(_out_kernel, alpha=alpha, ff_chunk=ff_chunk, n_blocks=n_blocks),
        grid=(n_blocks + 2,),
        in_specs=[row_in(d), row_in(og.shape[-1]), row_in(om.shape[-1]),
                  mod_spec(blk_in), mod_spec(blk_out),
                  _const_spec(wo.shape), _const_spec(ln.shape),
                  _const_spec(wg.shape), _const_spec(wu.shape), _const_spec(wd.shape)],
        out_specs=pl.BlockSpec((1, tm, d), lambda n: (blk_out(n) // nb, blk_out(n) % nb, 0)),
        out_shape=jax.ShapeDtypeStruct((bsz, s, d), F32),
        scratch_shapes=[pltpu.VMEM((tm, d_ff), BF16),
                        pltpu.VMEM((2, tm, d), F32),
                        pltpu.VMEM((2, tm, d), BF16),
                        pltpu.VMEM((2, tm, d), F32)],
        compiler_params=pltpu.CompilerParams(dimension_semantics=("arbitrary",),
                                             vmem_limit_bytes=VMEM_LIMIT),
        name="out_ffn",
    )(x, og, om, mod, mod, wo, ln, wg, wu, wd)


def _pack_w_in_tail(w_in):
    d = w_in.shape[0]
    split_z = DN_CONV_CH
    split_beta = split_z + DN_VW
    split_a = split_beta + DN_HEADS
    split_cq = split_a + DN_HEADS
    split_ckv = split_cq + Q_LORA
    split_kr = split_ckv + KV_LORA
    kr = w_in[:, split_kr:split_kr + QK_ROPE]
    zeros = lambda n: jnp.zeros((d, n), w_in.dtype)
    misc = jnp.concatenate([
        kr[:, :HALF_ROPE], zeros(KR2_LANE - HALF_ROPE),
        kr[:, HALF_ROPE:], zeros(BETA_LANE - KR2_LANE - HALF_ROPE),
        w_in[:, split_beta:split_a], w_in[:, split_a:split_cq],
        zeros(LANES - A_LANE - DN_HEADS)], axis=1)
    return jnp.concatenate([w_in[:, split_cq:split_kr], misc], axis=1).astype(BF16)


def _pack_w_uq(w_uq):
    q_lora = w_uq.shape[0]
    w = w_uq.reshape(q_lora, MLA_HEADS, QK_NOPE + QK_ROPE)
    zeros = lambda n: jnp.zeros((q_lora, MLA_HEADS, n), w_uq.dtype)
    tile = jnp.concatenate([
        w[:, :, :QK_NOPE],
        w[:, :, QK_NOPE:QK_NOPE + HALF_ROPE], zeros(KR2_LANE - HALF_ROPE),
        w[:, :, QK_NOPE + HALF_ROPE:], zeros(LANES - KR2_LANE - HALF_ROPE)], axis=2)
    return tile.reshape(q_lora, MLA_HEADS * QATT_W).astype(BF16)


def _pack_w_ukv(w_ukv):
    kv_lora = w_ukv.shape[0]
    w = w_ukv.reshape(kv_lora, MLA_HEADS, QK_NOPE + V_HEAD)
    return jnp.concatenate([w[:, :, :QK_NOPE].reshape(kv_lora, -1),
                            w[:, :, QK_NOPE:].reshape(kv_lora, -1)], axis=1).astype(BF16)


def _lane_table(a_log, dt_bias):
    z = jnp.zeros((LANES,), F32)
    put = lambda vec, lane: z.at[lane:lane + vec.shape[0]].set(vec)
    rows = [put(dt_bias.astype(F32), A_LANE), put(a_log.astype(F32), A_LANE)]
    rows += [z] * (SUBLANES - len(rows))
    return jnp.stack(rows)


def _inv_freq_column():
    inv_freq = 1.0 / (ROPE_THETA ** (jnp.arange(0, QK_ROPE, 2, dtype=F32) / QK_ROPE))
    return inv_freq.reshape(HALF_ROPE, 1)


def _layer(x, c_mod, pos, w_in, conv_w, a_log, dt_bias, dn_norm_g, q_norm_g, w_uq, kv_norm_g, w_ukv,
           w_o, ln1_g, ln1_b, w_gate, w_up, w_down, ln2_g, ln2_b, *, depth, tm, tg, gdn_group, tq, ff_chunk):
    alpha = (2.0 * depth) ** 0.25
    q_dn, k_dn, v_dn, zs, gb, q_att, k_att, vt = _inproj(
        x, c_mod, pos, _inv_freq_column(), w_in.astype(BF16), _pack_w_in_tail(w_in),
        conv_w.reshape(CONV_K, DN_CONV_CH).astype(F32),
        _lane_table(a_log, dt_bias), q_norm_g.reshape(1, -1).astype(F32), kv_norm_g.reshape(1, -1).astype(F32),
        _pack_w_uq(w_uq), _pack_w_ukv(w_ukv), tm=tq)
    og = _gdn(q_dn, k_dn, v_dn, gb, zs, dn_norm_g.reshape(1, -1).astype(F32), tg=tg, group=gdn_group)
    om = _attention(q_att, k_att, vt, tq=tq, nh=ATTN_HEADS_PER_STEP)
    ln = jnp.stack([ln1_g, ln1_b, ln2_g, ln2_b]).astype(F32)
    return _out_ffn(x, og, om, c_mod, w_o.astype(BF16), ln, w_gate.astype(BF16), w_up.astype(BF16),
                    w_down.astype(BF16), tm=tm, alpha=alpha, ff_chunk=ff_chunk)


def kernel(x, c, positions, w_ada, b_ada, w_in, conv_w, a_log, dt_bias, dn_norm_g, q_norm_g, w_uq, kv_norm_g, w_ukv, w_o, ln1_g, ln1_b, w_gate, w_up, w_down, ln2_g, ln2_b):
    bsz, s, d = x.shape
    depth = w_in.shape[0]
    tile = min(512, s)
    pos = positions.astype(F32).reshape(bsz, 1, s)
    for l in range(depth):
        mod = _modulation(c, w_ada[l], b_ada[l]).reshape(bsz, 6, d)
        x = _layer(x, mod, pos, w_in[l], conv_w[l], a_log[l], dt_bias[l], dn_norm_g[l], q_norm_g[l], w_uq[l],
                   kv_norm_g[l], w_ukv[l], w_o[l], ln1_g[l], ln1_b[l], w_gate[l], w_up[l], w_down[l],
                   ln2_g[l], ln2_b[l], depth=depth, tm=tile, tg=min(GDN_TILE, s), gdn_group=GDN_GROUP, tq=tile, ff_chunk=256)
    return x
```
